```python
import jax, jax.numpy as jnp
from jax import lax
import numpy as np

D_MODEL = 1024
BATCH = 16
SEQ = 2048
DEPTH = 1

HEAD_DIM = 64
N_HEADS_NA = 8
N_HEADS_DIL = 8
D_NA = N_HEADS_NA * HEAD_DIM
D_DIL = N_HEADS_DIL * HEAD_DIM
D_MIX = D_NA + D_DIL
GRID_W = 64
NA_ROWS_MAX = 8
NA_COLS = 16
DIL_CONFIGS = ((128, 1), (512, 4), (2048, 16))
Q_BLOCK = 128
PEER_HEADS = 8
PEER_NKEYS = 128
PEER_N = PEER_NKEYS * PEER_NKEYS
PEER_DKEY = 256
PEER_TOPK = 16
PEER_TOKEN_BLOCK = 128
EPS = 1e-6

kernel_name = "hybrid_na_dilated_peer_block"


def rmsnorm(x, g):
    xf = x.astype(jnp.float32)
    y = xf * lax.rsqrt(jnp.mean(xf * xf, axis=-1, keepdims=True) + EPS)
    return (y * g.astype(jnp.float32)).astype(x.dtype)


def alibi_slopes(n):
    return (2.0 ** (-8.0 * (np.arange(n) + 1) / n)).astype(np.float32)


def na_index(seq):
    rows = seq // GRID_W
    wr = min(NA_ROWS_MAX, rows)
    t = np.arange(seq)
    r, c = t // GRID_W, t % GRID_W
    r0 = np.clip(r - wr // 2, 0, rows - wr)
    c0 = np.clip(c - NA_COLS // 2, 0, GRID_W - NA_COLS)
    kr = r0[:, None, None] + np.arange(wr)[None, :, None]
    kc = c0[:, None, None] + np.arange(NA_COLS)[None, None, :]
    shape = (seq, wr, NA_COLS)
    idx = np.broadcast_to(kr * GRID_W + kc, shape).reshape(seq, -1)
    dr = np.broadcast_to(kr - r[:, None, None] + NA_ROWS_MAX - 1, shape).reshape(seq, -1)
    dc = np.broadcast_to(kc - c[:, None, None] + NA_COLS - 1, shape).reshape(seq, -1)
    return idx.astype(np.int32), dr.astype(np.int32), dc.astype(np.int32)


def dilated_index(seq, window, dilation):
    half = window // (2 * dilation)
    off = dilation * np.arange(-half, half + 1)
    pos = np.arange(seq)[:, None] + off[None, :]
    valid = (pos >= 0) & (pos < seq)
    idx = np.clip(pos, 0, seq - 1).astype(np.int32)
    return idx, valid, np.abs(off).astype(np.float32)


def gathered_attention(q, k, v, idx, bias, valid):
    B, H, S, hd = q.shape
    nblk = S // Q_BLOCK
    K = idx.shape[-1]
    scale = hd ** -0.5
    q_blocks = jnp.moveaxis(q.reshape(B, H, nblk, Q_BLOCK, hd), 2, 0)
    idx_blocks = idx.reshape(nblk, Q_BLOCK, K)
    bias_blocks = jnp.moveaxis(bias.reshape(H, nblk, Q_BLOCK, K), 1, 0)
    valid_blocks = valid.reshape(nblk, Q_BLOCK, K)

    def block(args):
        qb, ib, bb, mb = args
        kg = jnp.take(k, ib, axis=2)
        vg = jnp.take(v, ib, axis=2)
        s = jnp.einsum('bhqd,bhqkd->bhqk', qb, kg, preferred_element_type=jnp.float32) * scale + bb
        s = jnp.where(mb, s, -jnp.inf)
        lse = jax.nn.logsumexp(s, axis=-1)
        p = jnp.exp(s - lse[..., None])
        o = jnp.einsum('bhqk,bhqkd->bhqd', p.astype(v.dtype), vg)
        return o, lse

    o, lse = lax.map(block, (q_blocks, idx_blocks, bias_blocks, valid_blocks))
    o = jnp.moveaxis(o, 0, 2).reshape(B, H, S, hd)
    lse = jnp.moveaxis(lse, 0, 2).reshape(B, H, S)
    return o, lse


def peer(h, wq, subkeys, u_tab, v_tab):
    B, S, D = h.shape
    T = B * S
    hf = h.reshape(T, D)
    q = (hf @ wq).reshape(T, PEER_HEADS, 2, PEER_DKEY // 2)
    sc = jnp.einsum('thpd,pnd->thpn', q, subkeys, preferred_element_type=jnp.float32)
    top_v, top_i = lax.top_k(sc, PEER_TOPK)
    cand_v = (top_v[:, :, 0, :, None] + top_v[:, :, 1, None, :]).reshape(T, PEER_HEADS, -1)
    cand_id = (top_i[:, :, 0, :, None] * PEER_NKEYS + top_i[:, :, 1, None, :]).reshape(T, PEER_HEADS, -1)
    best_v, best_j = lax.top_k(cand_v, PEER_TOPK)
    ids = jnp.take_along_axis(cand_id, best_j, axis=-1)
    gates = jax.nn.softmax(best_v, axis=-1)
    E = PEER_HEADS * PEER_TOPK
    nblk = T // PEER_TOKEN_BLOCK
    x_blocks = hf.reshape(nblk, PEER_TOKEN_BLOCK, D)
    id_blocks = ids.reshape(nblk, PEER_TOKEN_BLOCK, E)
    g_blocks = gates.reshape(nblk, PEER_TOKEN_BLOCK, E)

    def block(args):
        xb, ib, gb = args
        ub = jnp.take(u_tab, ib, axis=0)
        a = jnp.einsum('td,ted->te', xb, ub, preferred_element_type=jnp.float32)
        w = gb * jax.nn.gelu(a, approximate=False)
        vb = jnp.take(v_tab, ib, axis=0)
        return jnp.einsum('te,ted->td', w.astype(vb.dtype), vb)

    y = lax.map(block, (x_blocks, id_blocks, g_blocks))
    return y.reshape(B, S, D)


def setup_inputs(seed: int = 0) -> dict:
    key = jax.random.key(seed)
    ks = jax.random.split(key, 16)
    f32 = jnp.float32
    L, D = DEPTH, D_MODEL

    def nrm(k, shape, std):
        return jax.random.normal(k, shape, f32) * std

    return {
        "x": nrm(ks[0], (BATCH, SEQ, D), 1.0),
        "c": nrm(ks[1], (BATCH, D), 1.0),
        "ada_w": nrm(ks[2], (L, D, 6 * D), 0.5 * D ** -0.5),
        "ada_b": nrm(ks[3], (L, 6 * D), 0.01),
        "norm1_g": 1.0 + nrm(ks[4], (L, D), 0.02),
        "w_in": nrm(ks[5], (L, D, 3 * D_MIX), D ** -0.5),
        "na_rpb": nrm(ks[6], (L, N_HEADS_NA, 2 * NA_ROWS_MAX - 1, 2 * NA_COLS - 1), 0.1),
        "out_norm_na_g": 1.0 + nrm(ks[7], (L, D_NA), 0.02),
        "out_norm_dil_g": 1.0 + nrm(ks[8], (L, D_DIL), 0.02),
        "w_out": nrm(ks[9], (L, D_MIX, D), D_MIX ** -0.5),
        "norm2_g": 1.0 + nrm(ks[10], (L, D), 0.02),
        "peer_wq": nrm(ks[11], (L, D, PEER_HEADS * PEER_DKEY), D ** -0.5),
        "peer_subkeys": nrm(ks[12], (L, 2, PEER_NKEYS, PEER_DKEY // 2), (PEER_DKEY // 2) ** -0.5),
        "peer_u": nrm(ks[13], (L, PEER_N, D), D ** -0.5),
        "peer_v": nrm(ks[14], (L, PEER_N, D), 1.0),
        "final_g": 1.0 + nrm(ks[15], (D,), 0.02),
    }


def reference(x, c, ada_w, ada_b, norm1_g, w_in, na_rpb, out_norm_na_g, out_norm_dil_g,
              w_out, norm2_g, peer_wq, peer_subkeys, peer_u, peer_v, final_g):
    B, S, D = x.shape

    na_idx_np, na_dr, na_dc = na_index(S)
    na_idx = jnp.asarray(na_idx_np)
    na_valid = jnp.ones(na_idx_np.shape, dtype=bool)
    slopes = alibi_slopes(N_HEADS_DIL)
    dil_tables = []
    for (window, dilation) in DIL_CONFIGS:
        idx_np, valid_np, dist = dilated_index(S, window, dilation)
        bias = np.broadcast_to((-slopes[:, None] * dist[None, :])[:, None, :],
                               (N_HEADS_DIL, S, idx_np.shape[1]))
        dil_tables.append((jnp.asarray(idx_np), jnp.asarray(np.ascontiguousarray(bias)), jnp.asarray(valid_np)))

    def heads(t, n):
        return t.reshape(B, S, n, HEAD_DIM).transpose(0, 2, 1, 3)

    def merge(t):
        return t.transpose(0, 2, 1, 3).reshape(B, S, -1)

    for l in range(DEPTH):
        mod = jax.nn.silu(c) @ ada_w[l] + ada_b[l]
        sh1, sc1, g1, sh2, sc2, g2 = jnp.split(mod[:, None, :], 6, axis=-1)

        h = rmsnorm(x, norm1_g[l]) * (1.0 + sc1) + sh1
        proj = h @ w_in[l]
        qa, ka, va, qd, kd, vd = jnp.split(
            proj, [D_NA, 2 * D_NA, 3 * D_NA, 3 * D_NA + D_DIL, 3 * D_NA + 2 * D_DIL], axis=-1)

        na_bias = na_rpb[l][:, na_dr, na_dc].astype(jnp.float32)
        o_na, _ = gathered_attention(heads(qa, N_HEADS_NA), heads(ka, N_HEADS_NA),
                                     heads(va, N_HEADS_NA), na_idx, na_bias, na_valid)

        qdh, kdh, vdh = heads(qd, N_HEADS_DIL), heads(kd, N_HEADS_DIL), heads(vd, N_HEADS_DIL)
        outs, lses = [], []
        for (idx_d, bias_d, valid_d) in dil_tables:
            o_i, lse_i = gathered_attention(qdh, kdh, vdh, idx_d, bias_d, valid_d)
            outs.append(o_i.astype(jnp.float32))
            lses.append(lse_i)
        alpha = jax.nn.softmax(jnp.stack(lses, 0), axis=0)
        o_dil = jnp.einsum('nbhs,nbhsd->bhsd', alpha, jnp.stack(outs, 0)).astype(x.dtype)

        y = jnp.concatenate([rmsnorm(merge(o_na), out_norm_na_g[l]),
                             rmsnorm(merge(o_dil), out_norm_dil_g[l])], axis=-1) @ w_out[l]
        x = x + g1 * y

        h2 = rmsnorm(x, norm2_g[l]) * (1.0 + sc2) + sh2
        x = x + g2 * peer(h2, peer_wq[l], peer_subkeys[l], peer_u[l], peer_v[l])

    return rmsnorm(x, final_g)
```

```python
import functools

import numpy as np
import jax
import jax.numpy as jnp
from jax import lax
from jax.experimental import pallas as pl
from jax.experimental.pallas import tpu as pltpu

F32 = jnp.float32
BF16 = jnp.bfloat16

D_MODEL = 1024
HEAD_DIM = 64
N_HEADS = 8
D_GRP = N_HEADS * HEAD_DIM
GRID_W = 64
NA_ROWS = 8
NA_COLS = 16
DIL_CONFIGS = ((128, 1), (512, 4), (2048, 16))
PEER_HEADS = 8
NKEYS = 128
TOPK = 16
EPS = 1e-6
MASKED = -1e30
LANES = 128
VMEM_LIMIT = 56 * 1024 * 1024

NA_QROWS = 4
NA_KROWS = 12
NA_QB = NA_QROWS * GRID_W
NA_KB = NA_KROWS * GRID_W
DIL_QB = 128
DIL_HALF = 64


def _cparams(*sem):
    return pltpu.CompilerParams(dimension_semantics=sem, vmem_limit_bytes=VMEM_LIMIT)


def _rms(x, g):
    ms = jnp.mean(x * x, axis=-1, keepdims=True)
    return x * lax.rsqrt(ms + EPS) * g


def _stack_heads(q):
    qf = q.astype(F32)
    lo = lax.broadcasted_iota(jnp.int32, qf.shape, 1) < HEAD_DIM
    return jnp.concatenate([jnp.where(lo, qf, 0.0), jnp.where(lo, 0.0, qf)], axis=0).astype(BF16)


def _dot_t(a, b):
    return lax.dot_general(a, b, (((1,), (1,)), ((), ())), preferred_element_type=F32)


def _mod_kernel(c_ref, w_ref, b_ref, o_ref):
    c = c_ref[...]
    s = c / (1.0 + jnp.exp(-c))
    o_ref[...] = jnp.dot(s, w_ref[...], preferred_element_type=F32,
                         precision=lax.Precision.HIGHEST) + b_ref[...]


def _modulation(c, ada_w, ada_b):
    B, D = c.shape
    n = ada_w.shape[1] // D
    return pl.pallas_call(
        _mod_kernel,
        grid=(n,),
        in_specs=[pl.BlockSpec((B, D), lambda j: (0, 0)),
                  pl.BlockSpec((D, D), lambda j: (0, j)),
                  pl.BlockSpec((1, D), lambda j: (0, j))],
        out_specs=pl.BlockSpec((B, D), lambda j: (0, j)),
        out_shape=jax.ShapeDtypeStruct((B, n * D), F32),
        compiler_params=_cparams("arbitrary"),
        name="mod",
    )(c, ada_w, ada_b.reshape(1, -1))


def _qkv_kernel(x_ref, mod_ref, g_ref, w_ref, *out_refs):
    sh1 = mod_ref[0, 0:1, :]
    sc1 = mod_ref[0, 1:2, :]
    h = _rms(x_ref[...], g_ref[...]) * (1.0 + sc1) + sh1
    hb = h.astype(BF16)
    for j, o_ref in enumerate(out_refs):
        p = jnp.dot(hb, w_ref[:, j * D_GRP:(j + 1) * D_GRP], preferred_element_type=F32)
        if j % 3 == 0:
            p = p * (HEAD_DIM ** -0.5)
        o_ref[...] = p.astype(BF16)


def _qkv(x2, mod3, g, w_in_b, S):
    T, D = x2.shape
    tm = 512
    per_b = S // tm
    return pl.pallas_call(
        _qkv_kernel,
        grid=(T // tm,),
        in_specs=[pl.BlockSpec((tm, D), lambda i: (i, 0)),
                  pl.BlockSpec((1, 6, D), lambda i: (i // per_b, 0, 0)),
                  pl.BlockSpec((1, D), lambda i: (0, 0)),
                  pl.BlockSpec((D, 6 * D_GRP), lambda i: (0, 0))],
        out_specs=[pl.BlockSpec((tm, D_GRP), lambda i: (i, 0))] * 6,
        out_shape=[jax.ShapeDtypeStruct((T, D_GRP), BF16)] * 6,
        compiler_params=_cparams("parallel"),
        name="qkv",
    )(x2, mod3, g, w_in_b)


def _na_bias_table(rpb, S):
    rows = S // GRID_W
    a = np.arange(NA_QROWS)[:, None, None, None]
    c = np.arange(GRID_W)[None, :, None, None]
    wr = np.arange(NA_KROWS)[None, None, :, None]
    kc = np.arange(GRID_W)[None, None, None, :]
    c0 = np.clip(c - NA_COLS // 2, 0, GRID_W - NA_COLS)
    tabs = []
    for R in (0, 1, rows // NA_QROWS - 1):
        r = NA_QROWS * R + a
        r0 = np.clip(r - NA_ROWS // 2, 0, rows - NA_ROWS)
        w0 = int(np.clip(NA_QROWS * R - NA_ROWS // 2, 0, rows - NA_KROWS))
        krow = w0 + wr
        valid = (krow >= r0) & (krow < r0 + NA_ROWS) & (kc >= c0) & (kc < c0 + NA_COLS)
        dr = np.clip(krow - r + NA_ROWS - 1, 0, 2 * NA_ROWS - 2)
        dc = np.clip(kc - c + NA_COLS - 1, 0, 2 * NA_COLS - 2)
        shape = (NA_QROWS, GRID_W, NA_KROWS, GRID_W)
        valid = np.broadcast_to(valid, shape).reshape(NA_QB, NA_KB)
        dr = np.broadcast_to(dr, shape).reshape(NA_QB, NA_KB)
        dc = np.broadcast_to(dc, shape).reshape(NA_QB, NA_KB)
        tabs.append(jnp.where(jnp.asarray(valid)[None], rpb[:, dr, dc].astype(F32), MASKED))
    return jnp.stack(tabs, 0)


def _na_kernel(q_ref, k0_ref, k1_ref, k2_ref, v0_ref, v1_ref, v2_ref, bias_ref, o_ref):
    k_refs = (k0_ref, k1_ref, k2_ref)
    v_refs = (v0_ref, v1_ref, v2_ref)
    lo = lax.broadcasted_iota(jnp.int32, (NA_QB, LANES), 1) < HEAD_DIM
    kb = NA_KB // 3
    for p in range(N_HEADS // 2):
        sl = slice(LANES * p, LANES * (p + 1))
        qs = _stack_heads(q_ref[0, :, sl])
        s = jnp.concatenate([_dot_t(qs, kr[0, :, sl]) for kr in k_refs], axis=1)
        s = s + jnp.concatenate([bias_ref[0, 2 * p], bias_ref[0, 2 * p + 1]], axis=0)
        m = jnp.max(s, axis=1, keepdims=True)
        e = jnp.exp(s - m)
        l = jnp.sum(e, axis=1, keepdims=True)
        eb = e.astype(BF16)
        o = jnp.dot(eb[:, 0:kb], v_refs[0][0, :, sl], preferred_element_type=F32)
        for j in (1, 2):
            o = o + jnp.dot(eb[:, j * kb:(j + 1) * kb], v_refs[j][0, :, sl],
                            preferred_element_type=F32)
        o = o / l
        o_ref[0, :, sl] = jnp.where(lo, o[:NA_QB], o[NA_QB:]).astype(BF16)


def _na_attention(q, k, v, bias, B, S):
    nblk = S // NA_QB
    kblk = NA_KB // 3
    assert kblk == NA_QB
    q3, k3, v3 = (t.reshape(B, S, D_GRP) for t in (q, k, v))

    def w0(R):
        return jnp.clip(R - 1, 0, nblk - 3)

    def kv_spec(j):
        return pl.BlockSpec((1, kblk, D_GRP), lambda R, b: (b, w0(R) + j, 0))

    return pl.pallas_call(
        _na_kernel,
        grid=(nblk, B),
        in_specs=[pl.BlockSpec((1, NA_QB, D_GRP), lambda R, b: (b, R, 0)),
                  kv_spec(0), kv_spec(1), kv_spec(2), kv_spec(0), kv_spec(1), kv_spec(2),
                  pl.BlockSpec((1, N_HEADS, NA_QB, NA_KB),
                               lambda R, b: (jnp.minimum(R, 1) + R // (nblk - 1), 0, 0, 0))],
        out_specs=pl.BlockSpec((1, NA_QB, D_GRP), lambda R, b: (b, R, 0)),
        out_shape=jax.ShapeDtypeStruct((B, S, D_GRP), BF16),
        compiler_params=_cparams("arbitrary", "arbitrary"),
        name="na",
    )(q3, k3, k3, k3, v3, v3, v3, bias)


def _alibi_slopes(n):
    return [float(2.0 ** (-8.0 * (i + 1) / n)) for i in range(n)]


def _dil_window(L):
    return min(L, 3 * DIL_QB)


def _dil_dist_table(L, dilation):
    wn = _dil_window(L)
    nqb = L // DIL_QB
    tab = np.empty((nqb, DIL_QB, wn), np.float32)
    for qb in range(nqb):
        ks = int(np.clip(qb * DIL_QB - DIL_QB, 0, L - wn))
        uq = qb * DIL_QB + np.arange(DIL_QB)[:, None]
        uk = ks + np.arange(wn)[None, :]
        delta = np.abs(uk - uq)
        tab[qb] = np.where(delta <= DIL_HALF, -float(dilation) * delta, MASKED)
    return jnp.asarray(tab)


def _dil_kernel(q_ref, k_ref, v_ref, d_ref, o_ref, lse_ref, *, L, slopes):
    wn = _dil_window(L)
    lane = lax.broadcasted_iota(jnp.int32, (DIL_QB, LANES), 1)
    lo = lane < HEAD_DIM

    def body(qb, carry):
        u0 = pl.multiple_of(qb * DIL_QB, DIL_QB)
        ks = pl.multiple_of(jnp.clip(u0 - DIL_QB, 0, L - wn), DIL_QB)
        dist = d_ref[qb]
        lse_tile = jnp.zeros((DIL_QB, LANES), F32)
        for p in range(N_HEADS // 2):
            sl = slice(LANES * p, LANES * (p + 1))
            qs = _stack_heads(q_ref[0, pl.ds(u0, DIL_QB), sl])
            s = _dot_t(qs, k_ref[0, pl.ds(ks, wn), sl])
            s = s + jnp.concatenate([slopes[2 * p] * dist, slopes[2 * p + 1] * dist], axis=0)
            m = jnp.max(s, axis=1, keepdims=True)
            e = jnp.exp(s - m)
            l = jnp.sum(e, axis=1, keepdims=True)
            o = jnp.dot(e.astype(BF16), v_ref[0, pl.ds(ks, wn), sl],
                        preferred_element_type=F32) / l
            lse = jnp.broadcast_to(m + jnp.log(l), (2 * DIL_QB, LANES))
            o_ref[0, pl.ds(u0, DIL_QB), sl] = jnp.where(lo, o[:DIL_QB], o[DIL_QB:])
            lse_tile = jnp.where(lane == 2 * p, lse[:DIL_QB], lse_tile)
            lse_tile = jnp.where(lane == 2 * p + 1, lse[DIL_QB:], lse_tile)
        lse_ref[0, pl.ds(u0, DIL_QB), :] = lse_tile
        return carry

    lax.fori_loop(0, L // DIL_QB, body, 0)


def _dil_branch(q, k, v, B, S, dilation):
    L = S // dilation
    wn = _dil_window(L)
    view = lambda t: t.reshape(B, L, dilation * D_GRP)
    spec = pl.BlockSpec((1, L, D_GRP), lambda b, rho: (b, 0, rho))
    dist = _dil_dist_table(L, dilation)
    o, lse = pl.pallas_call(
        functools.partial(_dil_kernel, L=L, slopes=_alibi_slopes(N_HEADS)),
        grid=(B, dilation),
        in_specs=[spec, spec, spec,
                  pl.BlockSpec((L // DIL_QB, DIL_QB, wn), lambda b, rho: (0, 0, 0))],
        out_specs=[spec, pl.BlockSpec((1, L, LANES), lambda b, rho: (b, 0, rho))],
        out_shape=[jax.ShapeDtypeStruct((B, L, dilation * D_GRP), F32),
                   jax.ShapeDtypeStruct((B, L, dilation * LANES), F32)],
        compiler_params=_cparams("parallel", "arbitrary"),
        name=f"dil{dilation}",
    )(view(q), view(k), view(v), dist)
    return o.reshape(B * S, D_GRP), lse.reshape(B * S, LANES)


def _mid_kernel(x_ref, ona_ref, o1_ref, o2_ref, o3_ref, l1_ref, l2_ref, l3_ref, mod_ref,
                gna_ref, gdil_ref, wout_ref, g2_ref, wqt_ref, sk_ref,
                x1_ref, h2t_ref, sct_ref):
    tm = x_ref.shape[0]
    l1, l2, l3 = l1_ref[...], l2_ref[...], l3_ref[...]
    mx = jnp.maximum(jnp.maximum(l1, l2), l3)
    e1, e2, e3 = jnp.exp(l1 - mx), jnp.exp(l2 - mx), jnp.exp(l3 - mx)
    inv = 1.0 / (e1 + e2 + e3)
    alphas = (e1 * inv, e2 * inv, e3 * inv)
    lo = lax.broadcasted_iota(jnp.int32, (tm, LANES), 1) < HEAD_DIM
    parts = []
    for p in range(N_HEADS // 2):
        sl = slice(LANES * p, LANES * (p + 1))
        acc = jnp.zeros((tm, LANES), F32)
        for a, o_ref in zip(alphas, (o1_ref, o2_ref, o3_ref)):
            w = jnp.where(lo, a[:, 2 * p:2 * p + 1], a[:, 2 * p + 1:2 * p + 2])
            acc = acc + w * o_ref[:, sl]
        parts.append(acc)
    o_dil = jnp.concatenate(parts, axis=1)

    na_n = _rms(ona_ref[...].astype(F32), gna_ref[...]).astype(BF16)
    dil_n = _rms(o_dil, gdil_ref[...]).astype(BF16)
    y = jnp.dot(na_n, wout_ref[0:D_GRP, :], preferred_element_type=F32)
    y = y + jnp.dot(dil_n, wout_ref[D_GRP:2 * D_GRP, :], preferred_element_type=F32)
    g1 = mod_ref[0, 2:3, :]
    sh2 = mod_ref[0, 3:4, :]
    sc2 = mod_ref[0, 4:5, :]
    x1 = x_ref[...] + g1 * y
    x1_ref[...] = x1
    h2 = _rms(x1, g2_ref[...]) * (1.0 + sc2) + sh2
    h2t = h2.T.astype(BF16)
    h2t_ref[...] = h2t
    qpt = jnp.dot(wqt_ref[...], h2t, preferred_element_type=F32).astype(BF16)
    for hp in range(2 * PEER_HEADS):
        rows = slice(NKEYS * hp, NKEYS * (hp + 1))
        sct_ref[rows, :] = jnp.dot(sk_ref[hp % 2], qpt[rows, :], preferred_element_type=F32)


def _mid(x2, o_na, dil_outs, mod3, g_na, g_dil, w_out_b, g2, wq_t, subk, S):
    T, D = x2.shape
    tm = 256
    per_b = S // tm
    nq = wq_t.shape[0]
    row = lambda w: pl.BlockSpec((tm, w), lambda i: (i, 0))
    full = lambda s: pl.BlockSpec(s, lambda i: (0,) * len(s))
    (o1, l1), (o2, l2), (o3, l3) = dil_outs
    return pl.pallas_call(
        _mid_kernel,
        grid=(T // tm,),
        in_specs=[row(D), row(D_GRP), row(D_GRP), row(D_GRP), row(D_GRP),
                  row(LANES), row(LANES), row(LANES),
                  pl.BlockSpec((1, 6, D), lambda i: (i // per_b, 0, 0)),
                  full((1, D_GRP)), full((1, D_GRP)), full((D, D)), full((1, D)),
                  full((nq, D)), full((2, NKEYS, NKEYS))],
        out_specs=[row(D),
                   pl.BlockSpec((D, tm), lambda i: (0, i)),
                   pl.BlockSpec((nq, tm), lambda i: (0, i))],
        out_shape=[jax.ShapeDtypeStruct((T, D), F32),
                   jax.ShapeDtypeStruct((D, T), BF16),
                   jax.ShapeDtypeStruct((nq, T), F32)],
        compiler_params=_cparams("parallel"),
        name="mid",
    )(x2, o_na, o1, o2, o3, l1, l2, l3, mod3, g_na, g_dil, w_out_b, g2, wq_t, subk)


NOT_TOP = 99.0


def _top16(s, iota):
    work = s
    rank = jnp.full(s.shape, NOT_TOP, F32)
    vals = []
    for r in range(TOPK):
        m = jnp.max(work, axis=0, keepdims=True)
        first = jnp.min(jnp.where(work == m, iota, float(s.shape[0])), axis=0, keepdims=True)
        sel = iota == first
        rank = jnp.where(sel, float(r), rank)
        work = jnp.where(sel, -jnp.inf, work)
        vals.append(m)
    return vals, rank


def _select_kernel(sc_ref, r2_ref, f_ref, lr_ref, c_ref):
    tb = sc_ref.shape[1]
    iota_k = lax.broadcasted_iota(jnp.int32, (NKEYS, tb), 0).astype(F32)
    iota_c = lax.broadcasted_iota(jnp.int32, (TOPK * TOPK, tb), 0).astype(F32)
    iota_t = lax.broadcasted_iota(jnp.int32, (TOPK, tb), 0)

    def head(h, carry):
        base = pl.multiple_of(h * 2 * NKEYS, 2 * NKEYS)
        s1 = sc_ref[pl.ds(base, NKEYS), :]
        s2 = sc_ref[pl.ds(base + NKEYS, NKEYS), :]
        a, rank1 = _top16(s1, iota_k)
        b, rank2 = _top16(s2, iota_k)
        b16 = jnp.zeros((TOPK, tb), F32)
        for cc in range(TOPK):
            b16 = jnp.where(iota_t == cc, b[cc], b16)
        cand = jnp.concatenate([a[r] + b16 for r in range(TOPK)], axis=0)
        work = cand
        picked = jnp.zeros(cand.shape, F32)
        for _ in range(TOPK):
            m = jnp.max(work, axis=0, keepdims=True)
            first = jnp.min(jnp.where(work == m, iota_c, float(TOPK * TOPK)), axis=0, keepdims=True)
            sel = iota_c == first
            picked = jnp.where(sel, 1.0, picked)
            work = jnp.where(sel, -jnp.inf, work)
        gate = picked * jnp.exp(cand - (a[0] + b[0]))
        z = jnp.sum(gate, axis=0, keepdims=True)
        lr = jnp.zeros((NKEYS, tb), F32)
        for r in range(TOPK):
            n_r = jnp.sum(picked[r * TOPK:(r + 1) * TOPK], axis=0, keepdims=True)
            lr = jnp.where(rank1 == float(r), n_r, lr)
        out = pl.multiple_of(h * NKEYS, NKEYS)
        r2_ref[pl.ds(out, NKEYS), :] = rank2
        f_ref[pl.ds(out, NKEYS), :] = jnp.exp(s2 - b[0])
        lr_ref[pl.ds(out, NKEYS), :] = lr
        c_ref[pl.ds(out, NKEYS), :] = jnp.exp(s1 - a[0]) / z
        return carry

    lax.fori_loop(0, PEER_HEADS, head, 0)


def _select(sct):
    nq, T = sct.shape
    tb = 256
    rows = PEER_HEADS * NKEYS
    out = jax.ShapeDtypeStruct((rows, T), F32)
    return pl.pallas_call(
        _select_kernel,
        grid=(T // tb,),
        in_specs=[pl.BlockSpec((nq, tb), lambda i: (0, i))],
        out_specs=[pl.BlockSpec((rows, tb), lambda i: (0, i))] * 4,
        out_shape=[out] * 4,
        compiler_params=_cparams("parallel"),
        name="select",
    )(sct)


PEER_ROWS_PER_STEP = 4


def _peer_kernel(h2t_ref, u_ref, vt_ref, r2_ref, f_ref, lr_ref, c_ref, x1_ref, mod_ref, fg_ref,
                 o_ref, acc_ref):
    n = pl.program_id(1)

    @pl.when(n == 0)
    def _():
        acc_ref[...] = jnp.zeros_like(acc_ref)

    a = jnp.dot(u_ref[...], h2t_ref[...], preferred_element_type=F32)
    act = 0.5 * a * (1.0 + lax.erf(a * (2.0 ** -0.5)))
    ws = []
    for k in range(PEER_ROWS_PER_STEP):
        i = n * PEER_ROWS_PER_STEP + k
        g = jnp.zeros((NKEYS, a.shape[1]), F32)
        for h in range(PEER_HEADS):
            lr = lr_ref[pl.ds(h * NKEYS + i, 1), :]
            cc = c_ref[pl.ds(h * NKEYS + i, 1), :]
            rows = slice(h * NKEYS, (h + 1) * NKEYS)
            g = g + jnp.where(r2_ref[rows, :] < lr, f_ref[rows, :], 0.0) * cc
        ws.append((g * act[k * NKEYS:(k + 1) * NKEYS]).astype(BF16))
    w = jnp.concatenate(ws, axis=0)
    acc_ref[...] += jnp.dot(vt_ref[...], w, preferred_element_type=F32)

    @pl.when(n == pl.num_programs(1) - 1)
    def _():
        g2 = mod_ref[0, 5:6, :]
        x2 = x1_ref[...] + g2 * acc_ref[...].T
        o_ref[...] = _rms(x2, fg_ref[...])


def _peer(h2t, u_b, vt_b, sel, x1, mod3, fg, S):
    D, T = h2t.shape
    tb = 256
    nc = PEER_ROWS_PER_STEP * NKEYS
    per_b = S // tb
    rows = PEER_HEADS * NKEYS
    tok = lambda r: pl.BlockSpec((r, tb), lambda t, n: (0, t))
    return pl.pallas_call(
        _peer_kernel,
        grid=(T // tb, NKEYS // PEER_ROWS_PER_STEP),
        in_specs=[tok(D),
                  pl.BlockSpec((nc, D), lambda t, n: (n, 0)),
                  pl.BlockSpec((D, nc), lambda t, n: (0, n)),
                  tok(rows), tok(rows), tok(rows), tok(rows),
                  pl.BlockSpec((tb, D), lambda t, n: (t, 0)),
                  pl.BlockSpec((1, 6, D), lambda t, n: (t // per_b, 0, 0)),
                  pl.BlockSpec((1, D), lambda t, n: (0, 0))],
        out_specs=pl.BlockSpec((tb, D), lambda t, n: (t, 0)),
        out_shape=jax.ShapeDtypeStruct((T, D), F32),
        scratch_shapes=[pltpu.VMEM((D, tb), F32)],
        compiler_params=_cparams("parallel", "arbitrary"),
        name="peer",
    )(h2t, u_b, vt_b, *sel, x1, mod3, fg)


def kernel(x, c, ada_w, ada_b, norm1_g, w_in, na_rpb, out_norm_na_g, out_norm_dil_g, w_out,
           norm2_g, peer_wq, peer_subkeys, peer_u, peer_v, final_g):
    B, S, D = x.shape
    assert ada_w.shape[0] == 1, "single-layer block"
    T = B * S
    x2 = x.reshape(T, D)

    mod3 = _modulation(c, ada_w[0], ada_b[0]).reshape(B, 6, D)
    qa, ka, va, qd, kd, vd = _qkv(x2, mod3, norm1_g[0].reshape(1, D), w_in[0].astype(BF16), S)

    o_na = _na_attention(qa, ka, va, _na_bias_table(na_rpb[0], S), B, S).reshape(T, D_GRP)
    dil_outs = [_dil_branch(qd, kd, vd, B, S, d) for (_, d) in DIL_CONFIGS]

    x1, h2t, sct = _mid(x2, o_na, dil_outs, mod3,
                        out_norm_na_g[0].reshape(1, D_GRP), out_norm_dil_g[0].reshape(1, D_GRP),
                        w_out[0].astype(BF16), norm2_g[0].reshape(1, D),
                        peer_wq[0].T.astype(BF16), peer_subkeys[0].astype(BF16), S)
    sel = _select(sct)
    out = _peer(h2t, peer_u[0].astype(BF16), peer_v[0].T.astype(BF16), sel, x1, mod3,
                final_g.reshape(1, D), S)
    return out.reshape(B, S, D)
```

```python
import functools

import numpy as np
import jax
import jax.numpy as jnp
from jax import lax
from jax.experimental import pallas as pl
from jax.experimental.pallas import tpu as pltpu

F32 = jnp.float32
BF16 = jnp.bfloat16

D_MODEL = 1024
HEAD_DIM = 64
N_HEADS = 8
D_GRP = N_HEADS * HEAD_DIM
GRID_W = 64
NA_ROWS = 8
NA_COLS = 16
DIL_CONFIGS = ((128, 1), (512, 4), (2048, 16))
PEER_HEADS = 8
NKEYS = 128
TOPK = 16
EPS = 1e-6
MASKED = -1e30
LANES = 128
VMEM_LIMIT = 56 * 1024 * 1024

NA_QROWS = 4
NA_KROWS = 12
NA_QB = NA_QROWS * GRID_W
NA_KB = NA_KROWS * GRID_W
DIL_QB = 128
DIL_HALF = 64


def _cparams(*sem):
    return pltpu.CompilerParams(dimension_semantics=sem, vmem_limit_bytes=VMEM_LIMIT)


def _rms(x, g):
    ms = jnp.mean(x * x, axis=-1, keepdims=True)
    return x * lax.rsqrt(ms + EPS) * g


def _stack_heads(q):
    qf = q.astype(F32)
    lo = lax.broadcasted_iota(jnp.int32, qf.shape, 1) < HEAD_DIM
    return jnp.concatenate([jnp.where(lo, qf, 0.0), jnp.where(lo, 0.0, qf)], axis=0).astype(BF16)


def _dot_t(a, b):
    return lax.dot_general(a, b, (((1,), (1,)), ((), ())), preferred_element_type=F32)


def _mod_kernel(c_ref, w_ref, b_ref, o_ref):
    c = c_ref[...]
    s = c / (1.0 + jnp.exp(-c))
    o_ref[...] = jnp.dot(s, w_ref[...], preferred_element_type=F32,
                         precision=lax.Precision.HIGHEST) + b_ref[...]


def _modulation(c, ada_w, ada_b):
    B, D = c.shape
    n = ada_w.shape[1] // D
    return pl.pallas_call(
        _mod_kernel,
        grid=(n,),
        in_specs=[pl.BlockSpec((B, D), lambda j: (0, 0)),
                  pl.BlockSpec((D, D), lambda j: (0, j)),
                  pl.BlockSpec((1, D), lambda j: (0, j))],
        out_specs=pl.BlockSpec((B, D), lambda j: (0, j)),
        out_shape=jax.ShapeDtypeStruct((B, n * D), F32),
        compiler_params=_cparams("arbitrary"),
        name="mod",
    )(c, ada_w, ada_b.reshape(1, -1))


def _qkv_kernel(x_ref, mod_ref, g_ref, w_ref, *out_refs):
    sh1 = mod_ref[0, 0:1, :]
    sc1 = mod_ref[0, 1:2, :]
    h = _rms(x_ref[...], g_ref[...]) * (1.0 + sc1) + sh1
    hb = h.astype(BF16)
    for j, o_ref in enumerate(out_refs):
        p = jnp.dot(hb, w_ref[:, j * D_GRP:(j + 1) * D_GRP], preferred_element_type=F32)
        if j % 3 == 0:
            p = p * (HEAD_DIM ** -0.5)
        o_ref[...] = p.astype(BF16)


def _qkv(x2, mod3, g, w_in_b, S):
    T, D = x2.shape
    tm = 512
    per_b = S // tm
    return pl.pallas_call(
        _qkv_kernel,
        grid=(T // tm,),
        in_specs=[pl.BlockSpec((tm, D), lambda i: (i, 0)),
                  pl.BlockSpec((1, 6, D), lambda i: (i // per_b, 0, 0)),
                  pl.BlockSpec((1, D), lambda i: (0, 0)),
                  pl.BlockSpec((D, 6 * D_GRP), lambda i: (0, 0))],
        out_specs=[pl.BlockSpec((tm, D_GRP), lambda i: (i, 0))] * 6,
        out_shape=[jax.ShapeDtypeStruct((T, D_GRP), BF16)] * 6,
        compiler_params=_cparams("parallel"),
        name="qkv",
    )(x2, mod3, g, w_in_b)


def _na_bias_table(rpb, S):
    rows = S // GRID_W
    n_dr, n_dc = 2 * NA_ROWS - 1, 2 * NA_COLS - 1
    c = np.arange(GRID_W)[:, None]
    kc = np.arange(GRID_W)[None, :]
    c0 = np.clip(c - NA_COLS // 2, 0, GRID_W - NA_COLS)
    col_ok = (kc >= c0) & (kc < c0 + NA_COLS)
    dc = np.clip(kc - c + NA_COLS - 1, 0, n_dc - 1)
    onehot = ((dc[None] == np.arange(n_dc)[:, None, None]) & col_ok[None]).astype(np.float32)
    blocks = jnp.einsum('hrd,dck->hrck', rpb.astype(F32), jnp.asarray(onehot),
                        precision=lax.Precision.HIGHEST)
    blocks = jnp.where(jnp.asarray(col_ok)[None, None], blocks, MASKED)
    masked_block = jnp.full((rpb.shape[0], 1, GRID_W, GRID_W), MASKED, F32)
    blocks = jnp.concatenate([blocks, masked_block], axis=1)
    a = np.arange(NA_QROWS)[:, None]
    wr = np.arange(NA_KROWS)[None, :]
    tabs = []
    for R in (0, 1, rows // NA_QROWS - 1):
        r = NA_QROWS * R + a
        r0 = np.clip(r - NA_ROWS // 2, 0, rows - NA_ROWS)
        w0 = int(np.clip(NA_QROWS * R - NA_ROWS // 2, 0, rows - NA_KROWS))
        krow = w0 + wr
        row_ok = (krow >= r0) & (krow < r0 + NA_ROWS)
        idx = np.where(row_ok, krow - r + NA_ROWS - 1, n_dr)
        t = jnp.stack([blocks[:, int(i)] for i in idx.reshape(-1)], axis=1)
        t = t.reshape(-1, NA_QROWS, NA_KROWS, GRID_W, GRID_W)
        tabs.append(t.transpose(0, 1, 3, 2, 4).reshape(-1, NA_QB, NA_KB))
    return jnp.stack(tabs, 0)


def _na_kernel(q_ref, k0_ref, k1_ref, k2_ref, v0_ref, v1_ref, v2_ref, bias_ref, o_ref):
    k_refs = (k0_ref, k1_ref, k2_ref)
    v_refs = (v0_ref, v1_ref, v2_ref)
    lo = lax.broadcasted_iota(jnp.int32, (NA_QB, LANES), 1) < HEAD_DIM
    kb = NA_KB // 3
    for p in range(N_HEADS // 2):
        sl = slice(LANES * p, LANES * (p + 1))
        qs = _stack_heads(q_ref[0, :, sl])
        s = jnp.concatenate([_dot_t(qs, kr[0, :, sl]) for kr in k_refs], axis=1)
        s = s + jnp.concatenate([bias_ref[0, 2 * p], bias_ref[0, 2 * p + 1]], axis=0)
        m = jnp.max(s, axis=1, keepdims=True)
        e = jnp.exp(s - m)
        l = jnp.sum(e, axis=1, keepdims=True)
        eb = e.astype(BF16)
        o = jnp.dot(eb[:, 0:kb], v_refs[0][0, :, sl], preferred_element_type=F32)
        for j in (1, 2):
            o = o + jnp.dot(eb[:, j * kb:(j + 1) * kb], v_refs[j][0, :, sl],
                            preferred_element_type=F32)
        o = o / l
        o_ref[0, :, sl] = jnp.where(lo, o[:NA_QB], o[NA_QB:]).astype(BF16)


def _na_attention(q, k, v, bias, B, S):
    nblk = S // NA_QB
    kblk = NA_KB // 3
    assert kblk == NA_QB
    q3, k3, v3 = (t.reshape(B, S, D_GRP) for t in (q, k, v))

    def w0(R):
        return jnp.clip(R - 1, 0, nblk - 3)

    def kv_spec(j):
        return pl.BlockSpec((1, kblk, D_GRP), lambda R, b: (b, w0(R) + j, 0))

    return pl.pallas_call(
        _na_kernel,
        grid=(nblk, B),
        in_specs=[pl.BlockSpec((1, NA_QB, D_GRP), lambda R, b: (b, R, 0)),
                  kv_spec(0), kv_spec(1), kv_spec(2), kv_spec(0), kv_spec(1), kv_spec(2),
                  pl.BlockSpec((1, N_HEADS, NA_QB, NA_KB),
                               lambda R, b: (jnp.minimum(R, 1) + R // (nblk - 1), 0, 0, 0))],
        out_specs=pl.BlockSpec((1, NA_QB, D_GRP), lambda R, b: (b, R, 0)),
        out_shape=jax.ShapeDtypeStruct((B, S, D_GRP), BF16),
        compiler_params=_cparams("arbitrary", "arbitrary"),
        name="na",
    )(q3, k3, k3, k3, v3, v3, v3, bias)


def _alibi_slopes(n):
    return [float(2.0 ** (-8.0 * (i + 1) / n)) for i in range(n)]


def _dil_window(L):
    return min(L, 3 * DIL_QB)


def _dil_dist_table(L, dilation):
    wn = _dil_window(L)
    nqb = L // DIL_QB
    tab = np.empty((nqb, DIL_QB, wn), np.float32)
    for qb in range(nqb):
        ks = int(np.clip(qb * DIL_QB - DIL_QB, 0, L - wn))
        uq = qb * DIL_QB + np.arange(DIL_QB)[:, None]
        uk = ks + np.arange(wn)[None, :]
        delta = np.abs(uk - uq)
        tab[qb] = np.where(delta <= DIL_HALF, -float(dilation) * delta, MASKED)
    return jnp.asarray(tab)


def _dil_kernel(q_ref, k_ref, v_ref, d_ref, o_ref, lse_ref, *, L, slopes):
    wn = _dil_window(L)
    lane = lax.broadcasted_iota(jnp.int32, (DIL_QB, LANES), 1)
    lo = lane < HEAD_DIM

    def body(qb, carry):
        u0 = pl.multiple_of(qb * DIL_QB, DIL_QB)
        ks = pl.multiple_of(jnp.clip(u0 - DIL_QB, 0, L - wn), DIL_QB)
        dist = d_ref[qb]
        lse_tile = jnp.zeros((DIL_QB, LANES), F32)
        for p in range(N_HEADS // 2):
            sl = slice(LANES * p, LANES * (p + 1))
            qs = _stack_heads(q_ref[0, pl.ds(u0, DIL_QB), sl])
            s = _dot_t(qs, k_ref[0, pl.ds(ks, wn), sl])
            s = s + jnp.concatenate([slopes[2 * p] * dist, slopes[2 * p + 1] * dist], axis=0)
            m = jnp.max(s, axis=1, keepdims=True)
            e = jnp.exp(s - m)
            l = jnp.sum(e, axis=1, keepdims=True)
            o = jnp.dot(e.astype(BF16), v_ref[0, pl.ds(ks, wn), sl],
                        preferred_element_type=F32) / l
            lse = jnp.broadcast_to(m + jnp.log(l), (2 * DIL_QB, LANES))
            o_ref[0, pl.ds(u0, DIL_QB), sl] = jnp.where(lo, o[:DIL_QB], o[DIL_QB:])
            lse_tile = jnp.where(lane == 2 * p, lse[:DIL_QB], lse_tile)
            lse_tile = jnp.where(lane == 2 * p + 1, lse[DIL_QB:], lse_tile)
        lse_ref[0, pl.ds(u0, DIL_QB), :] = lse_tile
        return carry

    lax.fori_loop(0, L // DIL_QB, body, 0)


def _dil_branch(q, k, v, B, S, dilation):
    L = S // dilation
    wn = _dil_window(L)
    view = lambda t: t.reshape(B, L, dilation * D_GRP)
    spec = pl.BlockSpec((1, L, D_GRP), lambda b, rho: (b, 0, rho))
    dist = _dil_dist_table(L, dilation)
    o, lse = pl.pallas_call(
        functools.partial(_dil_kernel, L=L, slopes=_alibi_slopes(N_HEADS)),
        grid=(B, dilation),
        in_specs=[spec, spec, spec,
                  pl.BlockSpec((L // DIL_QB, DIL_QB, wn), lambda b, rho: (0, 0, 0))],
        out_specs=[spec, pl.BlockSpec((1, L, LANES), lambda b, rho: (b, 0, rho))],
        out_shape=[jax.ShapeDtypeStruct((B, L, dilation * D_GRP), F32),
                   jax.ShapeDtypeStruct((B, L, dilation * LANES), F32)],
        compiler_params=_cparams("parallel", "arbitrary"),
        name=f"dil{dilation}",
    )(view(q), view(k), view(v), dist)
    return o.reshape(B * S, D_GRP), lse.reshape(B * S, LANES)


def _mid_kernel(x_ref, ona_ref, o1_ref, o2_ref, o3_ref, l1_ref, l2_ref, l3_ref, mod_ref,
                gna_ref, gdil_ref, wout_ref, g2_ref, wqt_ref, sk_ref,
                x1_ref, h2t_ref, sct_ref):
    tm = x_ref.shape[0]
    l1, l2, l3 = l1_ref[...], l2_ref[...], l3_ref[...]
    mx = jnp.maximum(jnp.maximum(l1, l2), l3)
    e1, e2, e3 = jnp.exp(l1 - mx), jnp.exp(l2 - mx), jnp.exp(l3 - mx)
    inv = 1.0 / (e1 + e2 + e3)
    alphas = (e1 * inv, e2 * inv, e3 * inv)
    lo = lax.broadcasted_iota(jnp.int32, (tm, LANES), 1) < HEAD_DIM
    parts = []
    for p in range(N_HEADS // 2):
        sl = slice(LANES * p, LANES * (p + 1))
        acc = jnp.zeros((tm, LANES), F32)
        for a, o_ref in zip(alphas, (o1_ref, o2_ref, o3_ref)):
            w = jnp.where(lo, a[:, 2 * p:2 * p + 1], a[:, 2 * p + 1:2 * p + 2])
            acc = acc + w * o_ref[:, sl]
        parts.append(acc)
    o_dil = jnp.concatenate(parts, axis=1)

    na_n = _rms(ona_ref[...].astype(F32), gna_ref[...]).astype(BF16)
    dil_n = _rms(o_dil, gdil_ref[...]).astype(BF16)
    y = jnp.dot(na_n, wout_ref[0:D_GRP, :], preferred_element_type=F32)
    y = y + jnp.dot(dil_n, wout_ref[D_GRP:2 * D_GRP, :], preferred_element_type=F32)
    g1 = mod_ref[0, 2:3, :]
    sh2 = mod_ref[0, 3:4, :]
    sc2 = mod_ref[0, 4:5, :]
    x1 = x_ref[...] + g1 * y
    x1_ref[...] = x1
    h2 = _rms(x1, g2_ref[...]) * (1.0 + sc2) + sh2
    h2t = h2.T.astype(BF16)
    h2t_ref[...] = h2t
    qpt = jnp.dot(wqt_ref[...], h2t, preferred_element_type=F32).astype(BF16)
    for hp in range(2 * PEER_HEADS):
        rows = slice(NKEYS * hp, NKEYS * (hp + 1))
        sct_ref[rows, :] = jnp.dot(sk_ref[hp % 2], qpt[rows, :], preferred_element_type=F32)


def _mid(x2, o_na, dil_outs, mod3, g_na, g_dil, w_out_b, g2, wq_t, subk, S):
    T, D = x2.shape
    tm = 256
    per_b = S // tm
    nq = wq_t.shape[0]
    row = lambda w: pl.BlockSpec((tm, w), lambda i: (i, 0))
    full = lambda s: pl.BlockSpec(s, lambda i: (0,) * len(s))
    (o1, l1), (o2, l2), (o3, l3) = dil_outs
    return pl.pallas_call(
        _mid_kernel,
        grid=(T // tm,),
        in_specs=[row(D), row(D_GRP), row(D_GRP), row(D_GRP), row(D_GRP),
                  row(LANES), row(LANES), row(LANES),
                  pl.BlockSpec((1, 6, D), lambda i: (i // per_b, 0, 0)),
                  full((1, D_GRP)), full((1, D_GRP)), full((D, D)), full((1, D)),
                  full((nq, D)), full((2, NKEYS, NKEYS))],
        out_specs=[row(D),
                   pl.BlockSpec((D, tm), lambda i: (0, i)),
                   pl.BlockSpec((nq, tm), lambda i: (0, i))],
        out_shape=[jax.ShapeDtypeStruct((T, D), F32),
                   jax.ShapeDtypeStruct((D, T), BF16),
                   jax.ShapeDtypeStruct((nq, T), F32)],
        compiler_params=_cparams("parallel"),
        name="mid",
    )(x2, o_na, o1, o2, o3, l1, l2, l3, mod3, g_na, g_dil, w_out_b, g2, wq_t, subk)


SUBLANES = 8
SEL_LANES = 128
NOT_TOP = float(TOPK)


def _merge_exchange_network(n):
    pairs = []
    t = (n - 1).bit_length()
    p = 1 << (t - 1)
    while p > 0:
        q, r, d = 1 << (t - 1), 0, p
        while d > 0:
            pairs += [(i, i + d) for i in range(n - d) if (i & p) == r]
            d, q, r = q - p, q >> 1, p
        p >>= 1
    return tuple(pairs)


def _bitonic_merge_network(n):
    pairs = []
    d = n // 2
    while d > 0:
        pairs += [(i, i + d) for i in range(n) if (i & d) == 0]
        d //= 2
    return tuple(pairs)


_SORT16 = _merge_exchange_network(TOPK)
_BITONIC16 = _bitonic_merge_network(TOPK)


def _compare_exchange(v, network):
    for i, j in network:
        v[i], v[j] = jnp.maximum(v[i], v[j]), jnp.minimum(v[i], v[j])


def _top16_values(v):
    v = list(v)
    _compare_exchange(v, _SORT16)
    for shift in (4, 2, 1):
        v = [jnp.maximum(v[k], pltpu.roll(v[TOPK - 1 - k], shift, axis=0)) for k in range(TOPK)]
        _compare_exchange(v, _BITONIC16)
    return v


def _all_sublanes_sum(x):
    for shift in (4, 2, 1):
        x = x + pltpu.roll(x, shift, axis=0)
    return x


def _select_tile_fast(s1, s2):
    n = s1[0].shape[1]
    sub = lax.broadcasted_iota(jnp.int32, (SUBLANES, n), 0)
    a = _top16_values(s1)
    b = _top16_values(s2)

    def spread(vals):
        out = vals[0]
        for k in range(1, SUBLANES):
            out = jnp.where(sub == k, vals[k], out)
        return out

    a_lo, a_hi = spread(a[:SUBLANES]), spread(a[SUBLANES:])
    b_lo, b_hi = spread(b[:SUBLANES]), spread(b[SUBLANES:])
    a_mid = jnp.where(sub < 4, -jnp.inf, a_lo)
    cand = []
    for r in range(4):
        cand += [a[r] + b_lo, a[r] + b_hi]
    cand += [a_mid + b[0], a_hi + b[0], a_mid + b[1], a_mid + b[2]]
    pad = [jnp.full((SUBLANES, n), -jnp.inf, F32)] * (TOPK - len(cand))
    thr = _top16_values(cand + pad)[TOPK - 1]
    m0 = a[0] + b[0]
    picked = [jnp.where(cv >= thr, 1.0, 0.0) for cv in cand]
    z = picked[0] * jnp.exp(cand[0] - m0)
    n_picked = picked[0]
    for pk, cv in zip(picked[1:], cand[1:]):
        z = z + pk * jnp.exp(cv - m0)
        n_picked = n_picked + pk
    inv_z = 1.0 / _all_sublanes_sum(z)
    counts = [_all_sublanes_sum(picked[2 * r] + picked[2 * r + 1]) for r in range(4)]
    n_mid = picked[8] + picked[10] + picked[11]
    counts += [_all_sublanes_sum(jnp.where(sub == r, n_mid, 0.0)) for r in range(4, 8)]
    counts += [_all_sublanes_sum(jnp.where(sub == r - 8, picked[9], 0.0)) for r in range(8, 16)]

    flag = jnp.where(_all_sublanes_sum(n_picked) != float(TOPK), 1.0, 0.0)
    for vals, keys in ((a, s1), (b, s2)):
        for r in range(TOPK - 1):
            flag = jnp.where(vals[r] == vals[r + 1], 1.0, flag)
        n_top = jnp.where(keys[0] >= vals[TOPK - 1], 1.0, 0.0)
        for x in keys[1:]:
            n_top = n_top + jnp.where(x >= vals[TOPK - 1], 1.0, 0.0)
        flag = jnp.where(_all_sublanes_sum(n_top) != float(TOPK), 1.0, flag)

    count, c = [], []
    for x in s1:
        cnt = jnp.zeros(x.shape, F32)
        for r in range(TOPK):
            cnt = jnp.where(x == a[r], counts[r], cnt)
        count.append(cnt)
        c.append(jnp.exp(x - a[0]) * inv_z)
    rank2, f = [], []
    for x in s2:
        rk = jnp.full(x.shape, NOT_TOP, F32)
        for cc in range(TOPK - 1, -1, -1):
            rk = jnp.where(b[cc] <= x, float(cc), rk)
        rank2.append(rk)
        f.append(jnp.exp(x - b[0]))
    return rank2, f, count, c, flag


def _top16_exact(s, iota):
    work = s
    rank = jnp.full(s.shape, NOT_TOP, F32)
    vals = []
    for r in range(TOPK):
        m = jnp.max(work, axis=0, keepdims=True)
        first = jnp.min(jnp.where(work == m, iota, float(s.shape[0])), axis=0, keepdims=True)
        sel = iota == first
        rank = jnp.where(sel, float(r), rank)
        work = jnp.where(sel, -jnp.inf, work)
        vals.append(m)
    return vals, rank


def _select_tile_exact(s1, s2):
    n = s1.shape[1]
    iota_k = lax.broadcasted_iota(jnp.int32, (NKEYS, n), 0).astype(F32)
    iota_c = lax.broadcasted_iota(jnp.int32, (TOPK * TOPK, n), 0).astype(F32)
    iota_t = lax.broadcasted_iota(jnp.int32, (TOPK, n), 0)
    a, rank1 = _top16_exact(s1, iota_k)
    b, rank2 = _top16_exact(s2, iota_k)
    b16 = jnp.zeros((TOPK, n), F32)
    for cc in range(TOPK):
        b16 = jnp.where(iota_t == cc, b[cc], b16)
    cand = jnp.concatenate([a[r] + b16 for r in range(TOPK)], axis=0)
    work = cand
    picked = jnp.zeros(cand.shape, F32)
    for _ in range(TOPK):
        m = jnp.max(work, axis=0, keepdims=True)
        first = jnp.min(jnp.where(work == m, iota_c, float(TOPK * TOPK)), axis=0, keepdims=True)
        sel = iota_c == first
        picked = jnp.where(sel, 1.0, picked)
        work = jnp.where(sel, -jnp.inf, work)
    z = jnp.sum(picked * jnp.exp(cand - (a[0] + b[0])), axis=0, keepdims=True)
    count = jnp.zeros((NKEYS, n), F32)
    for r in range(TOPK):
        n_r = jnp.sum(picked[r * TOPK:(r + 1) * TOPK], axis=0, keepdims=True)
        count = jnp.where(rank1 == float(r), n_r, count)
    return rank2, jnp.exp(s2 - b[0]), count, jnp.exp(s1 - a[0]) / z


def _select_kernel(sc_ref, r2_ref, f_ref, cnt_ref, c_ref):
    tb = sc_ref.shape[1]
    vregs = NKEYS // SUBLANES
    flag = jnp.zeros((SUBLANES, SEL_LANES), F32)
    for lt in range(tb // SEL_LANES):
        lanes = slice(lt * SEL_LANES, (lt + 1) * SEL_LANES)

        def fast(h, flag, lanes=lanes):
            base = pl.multiple_of(h * 2 * NKEYS, 2 * NKEYS)
            out = pl.multiple_of(h * NKEYS, NKEYS)
            s1 = [sc_ref[pl.ds(base + SUBLANES * k, SUBLANES), lanes] for k in range(vregs)]
            s2 = [sc_ref[pl.ds(base + NKEYS + SUBLANES * k, SUBLANES), lanes] for k in range(vregs)]
            rank2, f, count, c, bad = _select_tile_fast(s1, s2)
            for k in range(0, vregs, 2):
                rows = pl.ds(out + SUBLANES * k, 2 * SUBLANES)
                r2_ref[rows, lanes] = jnp.concatenate(rank2[k:k + 2], axis=0).astype(BF16)
                f_ref[rows, lanes] = jnp.concatenate(f[k:k + 2], axis=0).astype(BF16)
            for k in range(vregs):
                rows = pl.ds(out + SUBLANES * k, SUBLANES)
                cnt_ref[rows, lanes] = count[k]
                c_ref[rows, lanes] = c[k]
            return jnp.maximum(flag, bad)

        flag = lax.fori_loop(0, PEER_HEADS, fast, flag)

    @pl.when(jnp.max(flag) > 0.0)
    def _():
        for lt in range(tb // SEL_LANES):
            lanes = slice(lt * SEL_LANES, (lt + 1) * SEL_LANES)

            def exact(h, carry, lanes=lanes):
                base = pl.multiple_of(h * 2 * NKEYS, 2 * NKEYS)
                rows = pl.ds(pl.multiple_of(h * NKEYS, NKEYS), NKEYS)
                rank2, f, count, c = _select_tile_exact(sc_ref[pl.ds(base, NKEYS), lanes],
                                                        sc_ref[pl.ds(base + NKEYS, NKEYS), lanes])
                r2_ref[rows, lanes] = rank2.astype(BF16)
                f_ref[rows, lanes] = f.astype(BF16)
                cnt_ref[rows, lanes] = count
                c_ref[rows, lanes] = c
                return carry

            lax.fori_loop(0, PEER_HEADS, exact, 0)


def _select(sct):
    nq, T = sct.shape
    tb = 512
    rows = PEER_HEADS * NKEYS
    spec = pl.BlockSpec((rows, tb), lambda i: (0, i))
    return pl.pallas_call(
        _select_kernel,
        grid=(T // tb,),
        in_specs=[pl.BlockSpec((nq, tb), lambda i: (0, i))],
        out_specs=[spec] * 4,
        out_shape=[jax.ShapeDtypeStruct((rows, T), BF16)] * 2 + [jax.ShapeDtypeStruct((rows, T), F32)] * 2,
        compiler_params=_cparams("parallel"),
        name="select",
    )(sct)


PEER_ROWS_PER_STEP = 8
PEER_SUB_ROWS = 2
BF16_ROWS = 16


def _peer_kernel(h2t_ref, u_ref, vt_ref, r2_ref, f_ref, cnt_ref, c_ref, x1_ref, mod_ref, fg_ref,
                 o_ref, acc_ref, a_ref, w_ref):
    n = pl.program_id(1)
    tb = h2t_ref.shape[1]

    @pl.when(n == 0)
    def _():
        acc_ref[...] = jnp.zeros_like(acc_ref)

    sub = PEER_SUB_ROWS * NKEYS
    out = None
    for s in range(PEER_ROWS_PER_STEP // PEER_SUB_ROWS):
        chunk = slice(s * sub, (s + 1) * sub)
        a_ref[chunk, :] = jnp.dot(u_ref[chunk, :], h2t_ref[...], preferred_element_type=F32)
        for k in range(s * PEER_SUB_ROWS, (s + 1) * PEER_SUB_ROWS):
            cnt = [jnp.broadcast_to(cnt_ref[h, k:k + 1, :], (BF16_ROWS, tb)).astype(BF16)
                   for h in range(PEER_HEADS)]
            cc = [jnp.broadcast_to(c_ref[h, k:k + 1, :], (BF16_ROWS, tb)).astype(BF16)
                  for h in range(PEER_HEADS)]
            for q in range(NKEYS // BF16_ROWS):
                rows = slice(k * NKEYS + q * BF16_ROWS, k * NKEYS + (q + 1) * BF16_ROWS)
                av = a_ref[rows, :]
                act = (0.5 * av * (1.0 + lax.erf(av * (2.0 ** -0.5)))).astype(BF16)
                g = jnp.zeros((BF16_ROWS, tb), BF16)
                for h in range(PEER_HEADS):
                    keys = slice(h * NKEYS + q * BF16_ROWS, h * NKEYS + (q + 1) * BF16_ROWS)
                    g = g + jnp.where(r2_ref[keys, :] < cnt[h], f_ref[keys, :], 0.0) * cc[h]
                w_ref[rows, :] = g * act
        part = jnp.dot(vt_ref[:, chunk], w_ref[chunk, :], preferred_element_type=F32)
        out = part if out is None else out + part
    acc_ref[...] += out

    @pl.when(n == pl.num_programs(1) - 1)
    def _():
        g2 = mod_ref[0, 5:6, :]
        x2 = x1_ref[...] + g2 * acc_ref[...].T
        o_ref[...] = _rms(x2, fg_ref[...])


def _peer(h2t, u_b, vt_b, sel, x1, mod3, fg, S):
    D, T = h2t.shape
    tb = 512
    ic = PEER_ROWS_PER_STEP
    nc = ic * NKEYS
    per_b = S // tb
    rows = PEER_HEADS * NKEYS
    r2, f, cnt, c = sel
    tok = lambda r: pl.BlockSpec((r, tb), lambda t, n: (0, t))
    per_row = pl.BlockSpec((PEER_HEADS, ic, tb), lambda t, n: (0, n, t))
    return pl.pallas_call(
        _peer_kernel,
        grid=(T // tb, NKEYS // ic),
        in_specs=[tok(D),
                  pl.BlockSpec((nc, D), lambda t, n: (n, 0)),
                  pl.BlockSpec((D, nc), lambda t, n: (0, n)),
                  tok(rows), tok(rows), per_row, per_row,
                  pl.BlockSpec((tb, D), lambda t, n: (t, 0)),
                  pl.BlockSpec((1, 6, D), lambda t, n: (t // per_b, 0, 0)),
                  pl.BlockSpec((1, D), lambda t, n: (0, 0))],
        out_specs=pl.BlockSpec((tb, D), lambda t, n: (t, 0)),
        out_shape=jax.ShapeDtypeStruct((T, D), F32),
        scratch_shapes=[pltpu.VMEM((D, tb), F32), pltpu.VMEM((nc, tb), F32),
                        pltpu.VMEM((nc, tb), BF16)],
        compiler_params=_cparams("parallel", "arbitrary"),
        name="peer",
    )(h2t, u_b, vt_b, r2, f, cnt.reshape(PEER_HEADS, NKEYS, T), c.reshape(PEER_HEADS, NKEYS, T),
      x1, mod3, fg)


def kernel(x, c, ada_w, ada_b, norm1_g, w_in, na_rpb, out_norm_na_g, out_norm_dil_g, w_out,
           norm2_g, peer_wq, peer_subkeys, peer_u, peer_v, final_g):
    B, S, D = x.shape
    assert ada_w.shape[0] == 1, "single-layer block"
    T = B * S
    x2 = x.reshape(T, D)

    mod3 = _modulation(c, ada_w[0], ada_b[0]).reshape(B, 6, D)
    qa, ka, va, qd, kd, vd = _qkv(x2, mod3, norm1_g[0].reshape(1, D), w_in[0].astype(BF16), S)

    o_na = _na_attention(qa, ka, va, _na_bias_table(na_rpb[0], S), B, S).reshape(T, D_GRP)
    dil_outs = [_dil_branch(qd, kd, vd, B, S, d) for (_, d) in DIL_CONFIGS]

    x1, h2t, sct = _mid(x2, o_na, dil_outs, mod3,
                        out_norm_na_g[0].reshape(1, D_GRP), out_norm_dil_g[0].reshape(1, D_GRP),
                        w_out[0].astype(BF16), norm2_g[0].reshape(1, D),
                        peer_wq[0].T.astype(BF16), peer_subkeys[0].astype(BF16), S)
    sel = _select(sct)
    out = _peer(h2t, peer_u[0].astype(BF16), peer_v[0].T.astype(BF16), sel, x1, mod3,
                final_g.reshape(1, D), S)
    return out.reshape(B, S, D)
```

```python
import functools

import numpy as np
import jax
import jax.numpy as jnp
from jax import lax
from jax.experimental import pallas as pl
from jax.experimental.pallas import tpu as pltpu

F32 = jnp.float32
BF16 = jnp.bfloat16

D_MODEL = 1024
HEAD_DIM = 64
N_HEADS = 8
D_GRP = N_HEADS * HEAD_DIM
GRID_W = 64
NA_ROWS = 8
NA_COLS = 16
DIL_CONFIGS = ((128, 1), (512, 4), (2048, 16))
PEER_HEADS = 8
NKEYS = 128
TOPK = 16
EPS = 1e-6
MASKED = -1e30
LANES = 128
VMEM_LIMIT = 56 * 1024 * 1024

NA_QROWS = 4
NA_KROWS = 12
NA_QB = NA_QROWS * GRID_W
NA_KB = NA_KROWS * GRID_W
DIL_QB = 128
DIL_HALF = 64


def _cparams(*sem):
    return pltpu.CompilerParams(dimension_semantics=sem, vmem_limit_bytes=VMEM_LIMIT)


def _rms(x, g):
    ms = jnp.mean(x * x, axis=-1, keepdims=True)
    return x * lax.rsqrt(ms + EPS) * g


def _stack_heads(q):
    qf = q.astype(F32)
    lo = lax.broadcasted_iota(jnp.int32, qf.shape, 1) < HEAD_DIM
    return jnp.concatenate([jnp.where(lo, qf, 0.0), jnp.where(lo, 0.0, qf)], axis=0).astype(BF16)


def _dot_t(a, b):
    return lax.dot_general(a, b, (((1,), (1,)), ((), ())), preferred_element_type=F32)


def _mod_kernel(c_ref, w_ref, b_ref, o_ref):
    c = c_ref[...]
    s = c / (1.0 + jnp.exp(-c))
    o_ref[...] = jnp.dot(s, w_ref[...], preferred_element_type=F32,
                         precision=lax.Precision.HIGHEST) + b_ref[...]


def _modulation(c, ada_w, ada_b):
    B, D = c.shape
    n = ada_w.shape[1] // D
    return pl.pallas_call(
        _mod_kernel,
        grid=(n,),
        in_specs=[pl.BlockSpec((B, D), lambda j: (0, 0)),
                  pl.BlockSpec((D, D), lambda j: (0, j)),
                  pl.BlockSpec((1, D), lambda j: (0, j))],
        out_specs=pl.BlockSpec((B, D), lambda j: (0, j)),
        out_shape=jax.ShapeDtypeStruct((B, n * D), F32),
        compiler_params=_cparams("arbitrary"),
        name="mod",
    )(c, ada_w, ada_b.reshape(1, -1))


DILATIONS = tuple(d for _, d in DIL_CONFIGS)


def _qkv_kernel(x_ref, mod_ref, g_ref, w_ref, *refs):
    out_refs, stage_ref = refs[:-1], refs[-1]
    tm = x_ref.shape[0]
    sh1 = mod_ref[0, 0:1, :]
    sc1 = mod_ref[0, 1:2, :]
    h = _rms(x_ref[...], g_ref[...]) * (1.0 + sc1) + sh1
    hb = h.astype(BF16)
    for j in range(6):
        p = jnp.dot(hb, w_ref[:, j * D_GRP:(j + 1) * D_GRP], preferred_element_type=F32)
        if j % 3 == 0:
            p = p * (HEAD_DIM ** -0.5)
        if j < 3:
            out_refs[j][...] = p.astype(BF16)
            continue
        for c in range(D_GRP // LANES):
            stage_ref[c] = p[:, c * LANES:(c + 1) * LANES]
        for n, d in enumerate(DILATIONS):
            o_ref = out_refs[3 * n + j]
            for rho in range(d):
                for c in range(D_GRP // LANES):
                    col = rho * D_GRP + c * LANES
                    o_ref[0, :, col:col + LANES] = (
                        stage_ref[c, pl.ds(rho, tm // d, stride=d), :].astype(BF16))


def _qkv(x2, mod3, g, w_in_b, B, S):
    T, D = x2.shape
    tm = 512
    per_b = S // tm
    dil_specs, dil_shapes = [], []
    for d in DILATIONS:
        dil_specs += [pl.BlockSpec((1, tm // d, d * D_GRP), lambda i: (i // per_b, i % per_b, 0))] * 3
        dil_shapes += [jax.ShapeDtypeStruct((B, S // d, d * D_GRP), BF16)] * 3
    return pl.pallas_call(
        _qkv_kernel,
        grid=(T // tm,),
        in_specs=[pl.BlockSpec((tm, D), lambda i: (i, 0)),
                  pl.BlockSpec((1, 6, D), lambda i: (i // per_b, 0, 0)),
                  pl.BlockSpec((1, D), lambda i: (0, 0)),
                  pl.BlockSpec((D, 6 * D_GRP), lambda i: (0, 0))],
        out_specs=[pl.BlockSpec((tm, D_GRP), lambda i: (i, 0))] * 3 + dil_specs,
        out_shape=[jax.ShapeDtypeStruct((T, D_GRP), BF16)] * 3 + dil_shapes,
        scratch_shapes=[pltpu.VMEM((D_GRP // LANES, tm, LANES), F32)],
        compiler_params=_cparams("parallel"),
        name="qkv",
    )(x2, mod3, g, w_in_b)


def _na_bias_table(rpb, S):
    rows = S // GRID_W
    n_dr, n_dc = 2 * NA_ROWS - 1, 2 * NA_COLS - 1
    c = np.arange(GRID_W)[:, None]
    kc = np.arange(GRID_W)[None, :]
    c0 = np.clip(c - NA_COLS // 2, 0, GRID_W - NA_COLS)
    col_ok = (kc >= c0) & (kc < c0 + NA_COLS)
    dc = np.clip(kc - c + NA_COLS - 1, 0, n_dc - 1)
    onehot = ((dc[None] == np.arange(n_dc)[:, None, None]) & col_ok[None]).astype(np.float32)
    blocks = jnp.einsum('hrd,dck->hrck', rpb.astype(F32), jnp.asarray(onehot),
                        precision=lax.Precision.HIGHEST)
    blocks = jnp.where(jnp.asarray(col_ok)[None, None], blocks, MASKED)
    masked_block = jnp.full((rpb.shape[0], 1, GRID_W, GRID_W), MASKED, F32)
    blocks = jnp.concatenate([blocks, masked_block], axis=1)
    a = np.arange(NA_QROWS)[:, None]
    wr = np.arange(NA_KROWS)[None, :]
    idx = []
    for R in (0, 1, rows // NA_QROWS - 1):
        r = NA_QROWS * R + a
        r0 = np.clip(r - NA_ROWS // 2, 0, rows - NA_ROWS)
        w0 = int(np.clip(NA_QROWS * R - NA_ROWS // 2, 0, rows - NA_KROWS))
        krow = w0 + wr
        row_ok = (krow >= r0) & (krow < r0 + NA_ROWS)
        idx.append(np.where(row_ok, krow - r + NA_ROWS - 1, n_dr))
    t = jnp.take(blocks, jnp.asarray(np.stack(idx).reshape(-1), jnp.int32), axis=1)
    t = t.reshape(-1, 3, NA_QROWS, NA_KROWS, GRID_W, GRID_W)
    return t.transpose(1, 0, 2, 4, 3, 5).reshape(3, -1, NA_QB, NA_KB)


def _na_kernel(q_ref, k0_ref, k1_ref, k2_ref, v0_ref, v1_ref, v2_ref, bias_ref, o_ref):
    k_refs = (k0_ref, k1_ref, k2_ref)
    v_refs = (v0_ref, v1_ref, v2_ref)
    lo = lax.broadcasted_iota(jnp.int32, (NA_QB, LANES), 1) < HEAD_DIM
    kb = NA_KB // 3
    for p in range(N_HEADS // 2):
        sl = slice(LANES * p, LANES * (p + 1))
        qs = _stack_heads(q_ref[0, :, sl])
        s = jnp.concatenate([_dot_t(qs, kr[0, :, sl]) for kr in k_refs], axis=1)
        s = s + jnp.concatenate([bias_ref[0, 2 * p], bias_ref[0, 2 * p + 1]], axis=0)
        m = jnp.max(s, axis=1, keepdims=True)
        e = jnp.exp(s - m)
        l = jnp.sum(e, axis=1, keepdims=True)
        eb = e.astype(BF16)
        o = jnp.dot(eb[:, 0:kb], v_refs[0][0, :, sl], preferred_element_type=F32)
        for j in (1, 2):
            o = o + jnp.dot(eb[:, j * kb:(j + 1) * kb], v_refs[j][0, :, sl],
                            preferred_element_type=F32)
        o = o / l
        o_ref[0, :, sl] = jnp.where(lo, o[:NA_QB], o[NA_QB:]).astype(BF16)


def _na_attention(q, k, v, bias, B, S):
    nblk = S // NA_QB
    kblk = NA_KB // 3
    assert kblk == NA_QB
    q3, k3, v3 = (t.reshape(B, S, D_GRP) for t in (q, k, v))

    def w0(R):
        return jnp.clip(R - 1, 0, nblk - 3)

    def kv_spec(j):
        return pl.BlockSpec((1, kblk, D_GRP), lambda R, b: (b, w0(R) + j, 0))

    return pl.pallas_call(
        _na_kernel,
        grid=(nblk, B),
        in_specs=[pl.BlockSpec((1, NA_QB, D_GRP), lambda R, b: (b, R, 0)),
                  kv_spec(0), kv_spec(1), kv_spec(2), kv_spec(0), kv_spec(1), kv_spec(2),
                  pl.BlockSpec((1, N_HEADS, NA_QB, NA_KB),
                               lambda R, b: (jnp.minimum(R, 1) + R // (nblk - 1), 0, 0, 0))],
        out_specs=pl.BlockSpec((1, NA_QB, D_GRP), lambda R, b: (b, R, 0)),
        out_shape=jax.ShapeDtypeStruct((B, S, D_GRP), BF16),
        compiler_params=_cparams("arbitrary", "arbitrary"),
        name="na",
    )(q3, k3, k3, k3, v3, v3, v3, bias)


def _alibi_slopes(n):
    return [float(2.0 ** (-8.0 * (i + 1) / n)) for i in range(n)]


def _dil_window(L):
    return min(L, 3 * DIL_QB)


def _dil_dist_table(L, dilation):
    wn = _dil_window(L)
    nqb = L // DIL_QB
    tab = np.empty((nqb, DIL_QB, wn), np.float32)
    for qb in range(nqb):
        ks = int(np.clip(qb * DIL_QB - DIL_QB, 0, L - wn))
        uq = qb * DIL_QB + np.arange(DIL_QB)[:, None]
        uk = ks + np.arange(wn)[None, :]
        delta = np.abs(uk - uq)
        tab[qb] = np.where(delta <= DIL_HALF, -float(dilation) * delta, MASKED)
    return jnp.asarray(tab)


def _dil_kernel(q_ref, k_ref, v_ref, d_ref, o_ref, lse_ref, *, L, slopes):
    wn = _dil_window(L)
    lane = lax.broadcasted_iota(jnp.int32, (DIL_QB, LANES), 1)
    lo = lane < HEAD_DIM

    def body(qb, carry):
        u0 = pl.multiple_of(qb * DIL_QB, DIL_QB)
        ks = pl.multiple_of(jnp.clip(u0 - DIL_QB, 0, L - wn), DIL_QB)
        dist = d_ref[qb]
        lse_tile = jnp.zeros((DIL_QB, LANES), F32)
        for p in range(N_HEADS // 2):
            sl = slice(LANES * p, LANES * (p + 1))
            qs = _stack_heads(q_ref[0, pl.ds(u0, DIL_QB), sl])
            s = _dot_t(qs, k_ref[0, pl.ds(ks, wn), sl])
            s = s + jnp.concatenate([slopes[2 * p] * dist, slopes[2 * p + 1] * dist], axis=0)
            m = jnp.max(s, axis=1, keepdims=True)
            e = jnp.exp(s - m)
            l = jnp.sum(e, axis=1, keepdims=True)
            o = jnp.dot(e.astype(BF16), v_ref[0, pl.ds(ks, wn), sl],
                        preferred_element_type=F32) / l
            lse = jnp.broadcast_to(m + jnp.log(l), (2 * DIL_QB, LANES))
            o_ref[0, pl.ds(u0, DIL_QB), sl] = jnp.where(lo, o[:DIL_QB], o[DIL_QB:])
            lse_tile = jnp.where(lane == 2 * p, lse[:DIL_QB], lse_tile)
            lse_tile = jnp.where(lane == 2 * p + 1, lse[DIL_QB:], lse_tile)
        lse_ref[0, pl.ds(u0, DIL_QB), :] = lse_tile
        return carry

    lax.fori_loop(0, L // DIL_QB, body, 0)


def _dil_branch(q, k, v, B, S, dilation):
    L = S // dilation
    wn = _dil_window(L)
    spec = pl.BlockSpec((1, L, D_GRP), lambda b, rho: (b, 0, rho))
    dist = _dil_dist_table(L, dilation)
    return pl.pallas_call(
        functools.partial(_dil_kernel, L=L, slopes=_alibi_slopes(N_HEADS)),
        grid=(B, dilation),
        in_specs=[spec, spec, spec,
                  pl.BlockSpec((L // DIL_QB, DIL_QB, wn), lambda b, rho: (0, 0, 0))],
        out_specs=[spec, pl.BlockSpec((1, L, LANES), lambda b, rho: (b, 0, rho))],
        out_shape=[jax.ShapeDtypeStruct((B, L, dilation * D_GRP), F32),
                   jax.ShapeDtypeStruct((B, L, dilation * LANES), F32)],
        compiler_params=_cparams("parallel", "arbitrary"),
        name=f"dil{dilation}",
    )(q, k, v, dist)


def _mid_kernel(x_ref, ona_ref, o1_ref, o2_ref, o3_ref, l1_ref, l2_ref, l3_ref, mod_ref,
                gna_ref, gdil_ref, wout_ref, g2_ref, wqt_ref, sk_ref,
                x1_ref, h2t_ref, sct_ref, *stage_refs):
    tm = x_ref.shape[0]

    def token_order(ref, d, stage_ref):
        w = ref.shape[2] // d
        if d == 1:
            return [ref[0, :, c * LANES:(c + 1) * LANES] for c in range(w // LANES)]
        for rho in range(d):
            for c in range(w // LANES):
                col = rho * w + c * LANES
                stage_ref[c, pl.ds(rho, tm // d, stride=d), :] = ref[0, :, col:col + LANES]
        return [stage_ref[c] for c in range(w // LANES)]

    o_stage, l_stage = stage_refs[:len(DILATIONS)], stage_refs[len(DILATIONS):]
    o1, o2, o3 = (token_order(r, d, s) for r, d, s in zip((o1_ref, o2_ref, o3_ref), DILATIONS, o_stage))
    l1, l2, l3 = (token_order(r, d, s)[0] for r, d, s in zip((l1_ref, l2_ref, l3_ref), DILATIONS, l_stage))
    mx = jnp.maximum(jnp.maximum(l1, l2), l3)
    e1, e2, e3 = jnp.exp(l1 - mx), jnp.exp(l2 - mx), jnp.exp(l3 - mx)
    inv = 1.0 / (e1 + e2 + e3)
    alphas = (e1 * inv, e2 * inv, e3 * inv)
    lo = lax.broadcasted_iota(jnp.int32, (tm, LANES), 1) < HEAD_DIM
    parts = []
    for p in range(N_HEADS // 2):
        sl = slice(LANES * p, LANES * (p + 1))
        acc = jnp.zeros((tm, LANES), F32)
        for a, o in zip(alphas, (o1, o2, o3)):
            w = jnp.where(lo, a[:, 2 * p:2 * p + 1], a[:, 2 * p + 1:2 * p + 2])
            acc = acc + w * o[p]
        parts.append(acc)
    o_dil = jnp.concatenate(parts, axis=1)

    na_n = _rms(ona_ref[...].astype(F32), gna_ref[...]).astype(BF16)
    dil_n = _rms(o_dil, gdil_ref[...]).astype(BF16)
    y = jnp.dot(na_n, wout_ref[0:D_GRP, :], preferred_element_type=F32)
    y = y + jnp.dot(dil_n, wout_ref[D_GRP:2 * D_GRP, :], preferred_element_type=F32)
    g1 = mod_ref[0, 2:3, :]
    sh2 = mod_ref[0, 3:4, :]
    sc2 = mod_ref[0, 4:5, :]
    x1 = x_ref[...] + g1 * y
    x1_ref[...] = x1
    h2 = _rms(x1, g2_ref[...]) * (1.0 + sc2) + sh2
    h2t = h2.T.astype(BF16)
    h2t_ref[...] = h2t
    qpt = jnp.dot(wqt_ref[...], h2t, preferred_element_type=F32).astype(BF16)
    for hp in range(2 * PEER_HEADS):
        rows = slice(NKEYS * hp, NKEYS * (hp + 1))
        sct_ref[rows, :] = jnp.dot(sk_ref[hp % 2], qpt[rows, :], preferred_element_type=F32)


def _mid(x2, o_na, dil_outs, mod3, g_na, g_dil, w_out_b, g2, wq_t, subk, S):
    T, D = x2.shape
    tm = 256
    per_b = S // tm
    nq = wq_t.shape[0]
    row = lambda w: pl.BlockSpec((tm, w), lambda i: (i, 0))
    full = lambda s: pl.BlockSpec(s, lambda i: (0,) * len(s))
    view = lambda d, w: pl.BlockSpec((1, tm // d, d * w), lambda i: (i // per_b, i % per_b, 0))
    (o1, l1), (o2, l2), (o3, l3) = dil_outs
    return pl.pallas_call(
        _mid_kernel,
        grid=(T // tm,),
        in_specs=[row(D), row(D_GRP)] + [view(d, D_GRP) for d in DILATIONS]
                 + [view(d, LANES) for d in DILATIONS]
                 + [pl.BlockSpec((1, 6, D), lambda i: (i // per_b, 0, 0)),
                  full((1, D_GRP)), full((1, D_GRP)), full((D, D)), full((1, D)),
                  full((nq, D)), full((2, NKEYS, NKEYS))],
        out_specs=[row(D),
                   pl.BlockSpec((D, tm), lambda i: (0, i)),
                   pl.BlockSpec((nq, tm), lambda i: (0, i))],
        out_shape=[jax.ShapeDtypeStruct((T, D), F32),
                   jax.ShapeDtypeStruct((D, T), BF16),
                   jax.ShapeDtypeStruct((nq, T), F32)],
        scratch_shapes=[pltpu.VMEM((D_GRP // LANES, tm, LANES), F32)] * len(DILATIONS)
                       + [pltpu.VMEM((1, tm, LANES), F32)] * len(DILATIONS),
        compiler_params=_cparams("parallel"),
        name="mid",
    )(x2, o_na, o1, o2, o3, l1, l2, l3, mod3, g_na, g_dil, w_out_b, g2, wq_t, subk)


SUBLANES = 8
SEL_LANES = 128
NOT_TOP = float(TOPK)


def _merge_exchange_network(n):
    pairs = []
    t = (n - 1).bit_length()
    p = 1 << (t - 1)
    while p > 0:
        q, r, d = 1 << (t - 1), 0, p
        while d > 0:
            pairs += [(i, i + d) for i in range(n - d) if (i & p) == r]
            d, q, r = q - p, q >> 1, p
        p >>= 1
    return tuple(pairs)


def _bitonic_merge_network(n):
    pairs = []
    d = n // 2
    while d > 0:
        pairs += [(i, i + d) for i in range(n) if (i & d) == 0]
        d //= 2
    return tuple(pairs)


_SORT16 = _merge_exchange_network(TOPK)
_BITONIC16 = _bitonic_merge_network(TOPK)


def _compare_exchange(v, network):
    for i, j in network:
        v[i], v[j] = jnp.maximum(v[i], v[j]), jnp.minimum(v[i], v[j])


def _top16_values(v):
    v = list(v)
    _compare_exchange(v, _SORT16)
    for shift in (4, 2, 1):
        v = [jnp.maximum(v[k], pltpu.roll(v[TOPK - 1 - k], shift, axis=0)) for k in range(TOPK)]
        _compare_exchange(v, _BITONIC16)
    return v


def _all_sublanes_sum(x):
    for shift in (4, 2, 1):
        x = x + pltpu.roll(x, shift, axis=0)
    return x


def _select_tile_fast(s1, s2):
    n = s1[0].shape[1]
    sub = lax.broadcasted_iota(jnp.int32, (SUBLANES, n), 0)
    a = _top16_values(s1)
    b = _top16_values(s2)

    def spread(vals):
        out = vals[0]
        for k in range(1, SUBLANES):
            out = jnp.where(sub == k, vals[k], out)
        return out

    a_lo, a_hi = spread(a[:SUBLANES]), spread(a[SUBLANES:])
    b_lo, b_hi = spread(b[:SUBLANES]), spread(b[SUBLANES:])
    a_mid = jnp.where(sub < 4, -jnp.inf, a_lo)
    cand = []
    for r in range(4):
        cand += [a[r] + b_lo, a[r] + b_hi]
    cand += [a_mid + b[0], a_hi + b[0], a_mid + b[1], a_mid + b[2]]
    pad = [jnp.full((SUBLANES, n), -jnp.inf, F32)] * (TOPK - len(cand))
    thr = _top16_values(cand + pad)[TOPK - 1]
    m0 = a[0] + b[0]
    picked = [jnp.where(cv >= thr, 1.0, 0.0) for cv in cand]
    z = picked[0] * jnp.exp(cand[0] - m0)
    n_picked = picked[0]
    for pk, cv in zip(picked[1:], cand[1:]):
        z = z + pk * jnp.exp(cv - m0)
        n_picked = n_picked + pk
    inv_z = 1.0 / _all_sublanes_sum(z)
    counts = [_all_sublanes_sum(picked[2 * r] + picked[2 * r + 1]) for r in range(4)]
    n_mid = picked[8] + picked[10] + picked[11]
    counts += [_all_sublanes_sum(jnp.where(sub == r, n_mid, 0.0)) for r in range(4, 8)]
    counts += [_all_sublanes_sum(jnp.where(sub == r - 8, picked[9], 0.0)) for r in range(8, 16)]

    flag = jnp.where(_all_sublanes_sum(n_picked) != float(TOPK), 1.0, 0.0)
    for vals, keys, edge_used in ((a, s1, counts[TOPK - 1] > 0.0), (b, s2, counts[0] >= float(TOPK))):
        n_top = jnp.where(keys[0] >= vals[TOPK - 1], 1.0, 0.0)
        for x in keys[1:]:
            n_top = n_top + jnp.where(x >= vals[TOPK - 1], 1.0, 0.0)
        edge_tie = _all_sublanes_sum(n_top) != float(TOPK)
        flag = jnp.where(edge_tie & edge_used, 1.0, flag)

    count, c = [], []
    for x in s1:
        cnt = jnp.zeros(x.shape, F32)
        for r in range(TOPK):
            cnt = jnp.where(x == a[r], counts[r], cnt)
        count.append(cnt)
        c.append(jnp.exp(x - a[0]) * inv_z)
    rank2, f = [], []
    for x in s2:
        rk = jnp.full(x.shape, NOT_TOP, F32)
        for cc in range(TOPK - 1, -1, -1):
            rk = jnp.where(b[cc] <= x, float(cc), rk)
        rank2.append(rk)
        f.append(jnp.exp(x - b[0]))
    return rank2, f, count, c, flag


def _top16_exact(s, iota):
    work = s
    rank = jnp.full(s.shape, NOT_TOP, F32)
    vals = []
    for r in range(TOPK):
        m = jnp.max(work, axis=0, keepdims=True)
        first = jnp.min(jnp.where(work == m, iota, float(s.shape[0])), axis=0, keepdims=True)
        sel = iota == first
        rank = jnp.where(sel, float(r), rank)
        work = jnp.where(sel, -jnp.inf, work)
        vals.append(m)
    return vals, rank


def _select_tile_exact(s1, s2):
    n = s1.shape[1]
    iota_k = lax.broadcasted_iota(jnp.int32, (NKEYS, n), 0).astype(F32)
    iota_c = lax.broadcasted_iota(jnp.int32, (TOPK * TOPK, n), 0).astype(F32)
    iota_t = lax.broadcasted_iota(jnp.int32, (TOPK, n), 0)
    a, rank1 = _top16_exact(s1, iota_k)
    b, rank2 = _top16_exact(s2, iota_k)
    b16 = jnp.zeros((TOPK, n), F32)
    for cc in range(TOPK):
        b16 = jnp.where(iota_t == cc, b[cc], b16)
    cand = jnp.concatenate([a[r] + b16 for r in range(TOPK)], axis=0)
    work = cand
    picked = jnp.zeros(cand.shape, F32)
    for _ in range(TOPK):
        m = jnp.max(work, axis=0, keepdims=True)
        first = jnp.min(jnp.where(work == m, iota_c, float(TOPK * TOPK)), axis=0, keepdims=True)
        sel = iota_c == first
        picked = jnp.where(sel, 1.0, picked)
        work = jnp.where(sel, -jnp.inf, work)
    z = jnp.sum(picked * jnp.exp(cand - (a[0] + b[0])), axis=0, keepdims=True)
    count = jnp.zeros((NKEYS, n), F32)
    for r in range(TOPK):
        n_r = jnp.sum(picked[r * TOPK:(r + 1) * TOPK], axis=0, keepdims=True)
        count = jnp.where(rank1 == float(r), n_r, count)
    return rank2, jnp.exp(s2 - b[0]), count, jnp.exp(s1 - a[0]) / z


def _select_kernel(sc_ref, r2_ref, f_ref, cnt_ref, c_ref):
    tb = sc_ref.shape[1]
    vregs = NKEYS // SUBLANES
    flags = []
    for lt in range(tb // SEL_LANES):
        lanes = slice(lt * SEL_LANES, (lt + 1) * SEL_LANES)

        def fast(h, flag, lanes=lanes):
            base = pl.multiple_of(h * 2 * NKEYS, 2 * NKEYS)
            out = pl.multiple_of(h * NKEYS, NKEYS)
            s1 = [sc_ref[pl.ds(base + SUBLANES * k, SUBLANES), lanes] for k in range(vregs)]
            s2 = [sc_ref[pl.ds(base + NKEYS + SUBLANES * k, SUBLANES), lanes] for k in range(vregs)]
            rank2, f, count, c, bad = _select_tile_fast(s1, s2)
            for k in range(0, vregs, 2):
                rows = pl.ds(out + SUBLANES * k, 2 * SUBLANES)
                r2_ref[rows, lanes] = jnp.concatenate(rank2[k:k + 2], axis=0).astype(BF16)
                f_ref[rows, lanes] = jnp.concatenate(f[k:k + 2], axis=0).astype(BF16)
            for k in range(vregs):
                rows = pl.ds(out + SUBLANES * k, SUBLANES)
                cnt_ref[rows, lanes] = count[k]
                c_ref[rows, lanes] = c[k]
            return jnp.maximum(flag, bad)

        flags.append(lax.fori_loop(0, PEER_HEADS, fast, jnp.zeros((SUBLANES, SEL_LANES), F32)))

    for lt in range(tb // SEL_LANES):
        lanes = slice(lt * SEL_LANES, (lt + 1) * SEL_LANES)

        @pl.when(jnp.max(flags[lt]) > 0.0)
        def _(lanes=lanes):

            def exact(h, carry, lanes=lanes):
                base = pl.multiple_of(h * 2 * NKEYS, 2 * NKEYS)
                rows = pl.ds(pl.multiple_of(h * NKEYS, NKEYS), NKEYS)
                rank2, f, count, c = _select_tile_exact(sc_ref[pl.ds(base, NKEYS), lanes],
                                                        sc_ref[pl.ds(base + NKEYS, NKEYS), lanes])
                r2_ref[rows, lanes] = rank2.astype(BF16)
                f_ref[rows, lanes] = f.astype(BF16)
                cnt_ref[rows, lanes] = count
                c_ref[rows, lanes] = c
                return carry

            lax.fori_loop(0, PEER_HEADS, exact, 0)


def _select(sct):
    nq, T = sct.shape
    tb = 512
    rows = PEER_HEADS * NKEYS
    spec = pl.BlockSpec((rows, tb), lambda i: (0, i))
    return pl.pallas_call(
        _select_kernel,
        grid=(T // tb,),
        in_specs=[pl.BlockSpec((nq, tb), lambda i: (0, i))],
        out_specs=[spec] * 4,
        out_shape=[jax.ShapeDtypeStruct((rows, T), BF16)] * 2 + [jax.ShapeDtypeStruct((rows, T), F32)] * 2,
        compiler_params=_cparams("parallel"),
        name="select",
    )(sct)


PEER_ROWS_PER_STEP = 16
PEER_SUB_ROWS = 4
PEER_LOOKAHEAD = 3
BF16_ROWS = 16
PEER_GATE_LANES = 256
PEER_HEAD_GROUP = 8


def _peer_kernel(h2t_ref, u_ref, vt_ref, r2_ref, f_ref, cnt_ref, c_ref, x1_ref, mod_ref, fg_ref,
                 o_ref, acc_ref, a_ref, w_ref):
    n = pl.program_id(1)
    tb = h2t_ref.shape[1]

    @pl.when(n == 0)
    def _():
        acc_ref[...] = jnp.zeros_like(acc_ref)

    sub = PEER_SUB_ROWS * NKEYS
    n_sub = PEER_ROWS_PER_STEP // PEER_SUB_ROWS
    chunk = lambda s: slice(s * sub, (s + 1) * sub)

    def scores(s):
        a_ref[chunk(s), :] = jnp.dot(u_ref[chunk(s), :], h2t_ref[...],
                                     preferred_element_type=F32)

    def gate(s):
        for k in range(s * PEER_SUB_ROWS, (s + 1) * PEER_SUB_ROWS):
            for lt in range(tb // PEER_GATE_LANES):
                lanes = slice(lt * PEER_GATE_LANES, (lt + 1) * PEER_GATE_LANES)
                bcast = lambda ref, h: jnp.broadcast_to(
                    ref[h, k:k + 1, lanes], (BF16_ROWS, PEER_GATE_LANES)).astype(BF16)
                for h0 in range(0, PEER_HEADS, PEER_HEAD_GROUP):
                    heads = range(h0, h0 + PEER_HEAD_GROUP)
                    cnt = {h: bcast(cnt_ref, h) for h in heads}
                    cc = {h: bcast(c_ref, h) for h in heads}
                    for q in range(NKEYS // BF16_ROWS):
                        rows = slice(k * NKEYS + q * BF16_ROWS, k * NKEYS + (q + 1) * BF16_ROWS)
                        g = w_ref[rows, lanes] if h0 else jnp.zeros((BF16_ROWS, PEER_GATE_LANES), BF16)
                        for h in heads:
                            keys = slice(h * NKEYS + q * BF16_ROWS, h * NKEYS + (q + 1) * BF16_ROWS)
                            g = g + jnp.where(r2_ref[keys, lanes] < cnt[h], f_ref[keys, lanes], 0.0) * cc[h]
                        if h0 + PEER_HEAD_GROUP == PEER_HEADS:
                            av = a_ref[rows, lanes]
                            g = g * (0.5 * av * (1.0 + lax.erf(av * (2.0 ** -0.5)))).astype(BF16)
                        w_ref[rows, lanes] = g

    def combine(s):
        acc_ref[...] += jnp.dot(vt_ref[:, chunk(s)], w_ref[chunk(s), :],
                                preferred_element_type=F32)

    for s in range(PEER_LOOKAHEAD):
        scores(s)
    for s in range(n_sub):
        if s + PEER_LOOKAHEAD < n_sub:
            scores(s + PEER_LOOKAHEAD)
        gate(s)
        combine(s)

    @pl.when(n == pl.num_programs(1) - 1)
    def _():
        g2 = mod_ref[0, 5:6, :]
        x2 = x1_ref[...] + g2 * acc_ref[...].T
        o_ref[...] = _rms(x2, fg_ref[...])


def _peer(h2t, u_b, vt_b, sel, x1, mod3, fg, S):
    D, T = h2t.shape
    tb = 512
    ic = PEER_ROWS_PER_STEP
    nc = ic * NKEYS
    per_b = S // tb
    rows = PEER_HEADS * NKEYS
    r2, f, cnt, c = sel
    tok = lambda r: pl.BlockSpec((r, tb), lambda t, n: (0, t))
    per_row = pl.BlockSpec((PEER_HEADS, ic, tb), lambda t, n: (0, n, t))
    return pl.pallas_call(
        _peer_kernel,
        grid=(T // tb, NKEYS // ic),
        in_specs=[tok(D),
                  pl.BlockSpec((nc, D), lambda t, n: (n, 0)),
                  pl.BlockSpec((D, nc), lambda t, n: (0, n)),
                  tok(rows), tok(rows), per_row, per_row,
                  pl.BlockSpec((tb, D), lambda t, n: (t, 0)),
                  pl.BlockSpec((1, 6, D), lambda t, n: (t // per_b, 0, 0)),
                  pl.BlockSpec((1, D), lambda t, n: (0, 0))],
        out_specs=pl.BlockSpec((tb, D), lambda t, n: (t, 0)),
        out_shape=jax.ShapeDtypeStruct((T, D), F32),
        scratch_shapes=[pltpu.VMEM((D, tb), F32), pltpu.VMEM((nc, tb), F32),
                        pltpu.VMEM((nc, tb), BF16)],
        compiler_params=_cparams("parallel", "arbitrary"),
        name="peer",
    )(h2t, u_b, vt_b, r2, f, cnt.reshape(PEER_HEADS, NKEYS, T), c.reshape(PEER_HEADS, NKEYS, T),
      x1, mod3, fg)


def kernel(x, c, ada_w, ada_b, norm1_g, w_in, na_rpb, out_norm_na_g, out_norm_dil_g, w_out,
           norm2_g, peer_wq, peer_subkeys, peer_u, peer_v, final_g):
    B, S, D = x.shape
    assert ada_w.shape[0] == 1, "single-layer block"
    T = B * S
    x2 = x.reshape(T, D)

    mod3 = _modulation(c, ada_w[0], ada_b[0]).reshape(B, 6, D)
    qkv = _qkv(x2, mod3, norm1_g[0].reshape(1, D), w_in[0].astype(BF16), B, S)
    qa, ka, va = qkv[:3]

    o_na = _na_attention(qa, ka, va, _na_bias_table(na_rpb[0], S), B, S).reshape(T, D_GRP)
    dil_outs = [_dil_branch(*qkv[3 + 3 * n:6 + 3 * n], B, S, d) for n, d in enumerate(DILATIONS)]

    x1, h2t, sct = _mid(x2, o_na, dil_outs, mod3,
                        out_norm_na_g[0].reshape(1, D_GRP), out_norm_dil_g[0].reshape(1, D_GRP),
                        w_out[0].astype(BF16), norm2_g[0].reshape(1, D),
                        peer_wq[0].T.astype(BF16), peer_subkeys[0].astype(BF16), S)
    sel = _select(sct)
    out = _peer(h2t, peer_u[0].astype(BF16), peer_v[0].T.astype(BF16), sel, x1, mod3,
                final_g.reshape(1, D), S)
    return out.reshape(B, S, D)
```

```python
import functools

import numpy as np
import jax
import jax.numpy as jnp
from jax import lax
from jax.experimental import pallas as pl
from jax.experimental.pallas import tpu as pltpu

F32 = jnp.float32
BF16 = jnp.bfloat16

D_MODEL = 1024
HEAD_DIM = 64
N_HEADS = 8
D_GRP = N_HEADS * HEAD_DIM
GRID_W = 64
NA_ROWS = 8
NA_COLS = 16
DIL_CONFIGS = ((128, 1), (512, 4), (2048, 16))
PEER_HEADS = 8
NKEYS = 128
TOPK = 16
EPS = 1e-6
MASKED = -1e30
LANES = 128
VMEM_LIMIT = 56 * 1024 * 1024

NA_QROWS = 4
NA_KROWS = 12
NA_QB = NA_QROWS * GRID_W
NA_KB = NA_KROWS * GRID_W
DIL_QB = 128
DIL_HALF = 64


def _cparams(*sem):
    return pltpu.CompilerParams(dimension_semantics=sem, vmem_limit_bytes=VMEM_LIMIT)


def _rms(x, g):
    ms = jnp.mean(x * x, axis=-1, keepdims=True)
    return x * lax.rsqrt(ms + EPS) * g


def _stack_heads(q):
    qf = q.astype(F32)
    lo = lax.broadcasted_iota(jnp.int32, qf.shape, 1) < HEAD_DIM
    return jnp.concatenate([jnp.where(lo, qf, 0.0), jnp.where(lo, 0.0, qf)], axis=0).astype(BF16)


def _dot_t(a, b):
    return lax.dot_general(a, b, (((1,), (1,)), ((), ())), preferred_element_type=F32)


def _mod_kernel(c_ref, w_ref, b_ref, o_ref):
    c = c_ref[...]
    s = c / (1.0 + jnp.exp(-c))
    o_ref[...] = jnp.dot(s, w_ref[...], preferred_element_type=F32,
                         precision=lax.Precision.HIGHEST) + b_ref[...]


def _modulation(c, ada_w, ada_b):
    B, D = c.shape
    n = ada_w.shape[1] // D
    return pl.pallas_call(
        _mod_kernel,
        grid=(n,),
        in_specs=[pl.BlockSpec((B, D), lambda j: (0, 0)),
                  pl.BlockSpec((D, D), lambda j: (0, j)),
                  pl.BlockSpec((1, D), lambda j: (0, j))],
        out_specs=pl.BlockSpec((B, D), lambda j: (0, j)),
        out_shape=jax.ShapeDtypeStruct((B, n * D), F32),
        compiler_params=_cparams("arbitrary"),
        name="mod",
    )(c, ada_w, ada_b.reshape(1, -1))


DILATIONS = tuple(d for _, d in DIL_CONFIGS)


def _qkv_kernel(x_ref, mod_ref, g_ref, w_ref, *refs):
    out_refs, stage_ref = refs[:-1], refs[-1]
    tm = x_ref.shape[0]
    sh1 = mod_ref[0, 0:1, :]
    sc1 = mod_ref[0, 1:2, :]
    h = _rms(x_ref[...], g_ref[...]) * (1.0 + sc1) + sh1
    hb = h.astype(BF16)
    for j in range(6):
        p = jnp.dot(hb, w_ref[:, j * D_GRP:(j + 1) * D_GRP], preferred_element_type=F32)
        if j % 3 == 0:
            p = p * (HEAD_DIM ** -0.5)
        if j < 3:
            out_refs[j][...] = p.astype(BF16)
            continue
        for c in range(D_GRP // LANES):
            stage_ref[c] = p[:, c * LANES:(c + 1) * LANES]
        for n, d in enumerate(DILATIONS):
            o_ref = out_refs[3 * n + j]
            for rho in range(d):
                for c in range(D_GRP // LANES):
                    col = rho * D_GRP + c * LANES
                    o_ref[0, :, col:col + LANES] = (
                        stage_ref[c, pl.ds(rho, tm // d, stride=d), :].astype(BF16))


def _qkv(x2, mod3, g, w_in_b, B, S):
    T, D = x2.shape
    tm = 512
    per_b = S // tm
    dil_specs, dil_shapes = [], []
    for d in DILATIONS:
        dil_specs += [pl.BlockSpec((1, tm // d, d * D_GRP), lambda i: (i // per_b, i % per_b, 0))] * 3
        dil_shapes += [jax.ShapeDtypeStruct((B, S // d, d * D_GRP), BF16)] * 3
    return pl.pallas_call(
        _qkv_kernel,
        grid=(T // tm,),
        in_specs=[pl.BlockSpec((tm, D), lambda i: (i, 0)),
                  pl.BlockSpec((1, 6, D), lambda i: (i // per_b, 0, 0)),
                  pl.BlockSpec((1, D), lambda i: (0, 0)),
                  pl.BlockSpec((D, 6 * D_GRP), lambda i: (0, 0))],
        out_specs=[pl.BlockSpec((tm, D_GRP), lambda i: (i, 0))] * 3 + dil_specs,
        out_shape=[jax.ShapeDtypeStruct((T, D_GRP), BF16)] * 3 + dil_shapes,
        scratch_shapes=[pltpu.VMEM((D_GRP // LANES, tm, LANES), F32)],
        compiler_params=_cparams("parallel"),
        name="qkv",
    )(x2, mod3, g, w_in_b)


def _na_bias_table(rpb, S):
    rows = S // GRID_W
    n_dr, n_dc = 2 * NA_ROWS - 1, 2 * NA_COLS - 1
    c = np.arange(GRID_W)[:, None]
    kc = np.arange(GRID_W)[None, :]
    c0 = np.clip(c - NA_COLS // 2, 0, GRID_W - NA_COLS)
    col_ok = (kc >= c0) & (kc < c0 + NA_COLS)
    dc = np.clip(kc - c + NA_COLS - 1, 0, n_dc - 1)
    onehot = ((dc[None] == np.arange(n_dc)[:, None, None]) & col_ok[None]).astype(np.float32)
    blocks = jnp.einsum('hrd,dck->hrck', rpb.astype(F32), jnp.asarray(onehot),
                        precision=lax.Precision.HIGHEST)
    blocks = jnp.where(jnp.asarray(col_ok)[None, None], blocks, MASKED)
    masked_block = jnp.full((rpb.shape[0], 1, GRID_W, GRID_W), MASKED, F32)
    blocks = jnp.concatenate([blocks, masked_block], axis=1)
    a = np.arange(NA_QROWS)[:, None]
    wr = np.arange(NA_KROWS)[None, :]
    idx = []
    for R in (0, 1, rows // NA_QROWS - 1):
        r = NA_QROWS * R + a
        r0 = np.clip(r - NA_ROWS // 2, 0, rows - NA_ROWS)
        w0 = int(np.clip(NA_QROWS * R - NA_ROWS // 2, 0, rows - NA_KROWS))
        krow = w0 + wr
        row_ok = (krow >= r0) & (krow < r0 + NA_ROWS)
        idx.append(np.where(row_ok, krow - r + NA_ROWS - 1, n_dr))
    t = jnp.take(blocks, jnp.asarray(np.stack(idx).reshape(-1), jnp.int32), axis=1)
    t = t.reshape(-1, 3, NA_QROWS, NA_KROWS, GRID_W, GRID_W)
    return t.transpose(1, 0, 2, 4, 3, 5).reshape(3, -1, NA_QB, NA_KB)


def _na_kernel(q_ref, k0_ref, k1_ref, k2_ref, v0_ref, v1_ref, v2_ref, bias_ref, o_ref):
    k_refs = (k0_ref, k1_ref, k2_ref)
    v_refs = (v0_ref, v1_ref, v2_ref)
    lo = lax.broadcasted_iota(jnp.int32, (NA_QB, LANES), 1) < HEAD_DIM
    kb = NA_KB // 3
    for p in range(N_HEADS // 2):
        sl = slice(LANES * p, LANES * (p + 1))
        qs = _stack_heads(q_ref[0, :, sl])
        s = jnp.concatenate([_dot_t(qs, kr[0, :, sl]) for kr in k_refs], axis=1)
        s = s + jnp.concatenate([bias_ref[0, 2 * p], bias_ref[0, 2 * p + 1]], axis=0)
        m = jnp.max(s, axis=1, keepdims=True)
        e = jnp.exp(s - m)
        l = jnp.sum(e, axis=1, keepdims=True)
        eb = e.astype(BF16)
        o = jnp.dot(eb[:, 0:kb], v_refs[0][0, :, sl], preferred_element_type=F32)
        for j in (1, 2):
            o = o + jnp.dot(eb[:, j * kb:(j + 1) * kb], v_refs[j][0, :, sl],
                            preferred_element_type=F32)
        o = o / l
        o_ref[0, :, sl] = jnp.where(lo, o[:NA_QB], o[NA_QB:]).astype(BF16)


def _na_attention(q, k, v, bias, B, S):
    nblk = S // NA_QB
    kblk = NA_KB // 3
    assert kblk == NA_QB
    q3, k3, v3 = (t.reshape(B, S, D_GRP) for t in (q, k, v))

    def w0(R):
        return jnp.clip(R - 1, 0, nblk - 3)

    def kv_spec(j):
        return pl.BlockSpec((1, kblk, D_GRP), lambda R, b: (b, w0(R) + j, 0))

    return pl.pallas_call(
        _na_kernel,
        grid=(nblk, B),
        in_specs=[pl.BlockSpec((1, NA_QB, D_GRP), lambda R, b: (b, R, 0)),
                  kv_spec(0), kv_spec(1), kv_spec(2), kv_spec(0), kv_spec(1), kv_spec(2),
                  pl.BlockSpec((1, N_HEADS, NA_QB, NA_KB),
                               lambda R, b: (jnp.minimum(R, 1) + R // (nblk - 1), 0, 0, 0))],
        out_specs=pl.BlockSpec((1, NA_QB, D_GRP), lambda R, b: (b, R, 0)),
        out_shape=jax.ShapeDtypeStruct((B, S, D_GRP), BF16),
        compiler_params=_cparams("arbitrary", "arbitrary"),
        name="na",
    )(q3, k3, k3, k3, v3, v3, v3, bias)


def _alibi_slopes(n):
    return [float(2.0 ** (-8.0 * (i + 1) / n)) for i in range(n)]


def _dil_window(L):
    return min(L, DIL_QB + 2 * DIL_HALF)


def _dil_window_start(u0, L, clip):
    return clip(u0 - DIL_HALF, 0, L - _dil_window(L))


def _dil_dist_table(L, dilation):
    wn = _dil_window(L)
    nqb = L // DIL_QB
    tab = np.empty((nqb, DIL_QB, wn), np.float32)
    for qb in range(nqb):
        ks = int(_dil_window_start(qb * DIL_QB, L, np.clip))
        uq = qb * DIL_QB + np.arange(DIL_QB)[:, None]
        uk = ks + np.arange(wn)[None, :]
        delta = np.abs(uk - uq)
        tab[qb] = np.where(delta <= DIL_HALF, -float(dilation) * delta, MASKED)
    return jnp.asarray(tab)


def _dil_kernel(q_ref, k_ref, v_ref, d_ref, o_ref, lse_ref, *, L, slopes):
    wn = _dil_window(L)
    lane = lax.broadcasted_iota(jnp.int32, (DIL_QB, LANES), 1)
    lo = lane < HEAD_DIM

    def body(qb, carry):
        u0 = pl.multiple_of(qb * DIL_QB, DIL_QB)
        ks = pl.multiple_of(_dil_window_start(u0, L, jnp.clip), DIL_HALF)
        dist = d_ref[qb]
        lse_tile = jnp.zeros((DIL_QB, LANES), F32)
        for p in range(N_HEADS // 2):
            sl = slice(LANES * p, LANES * (p + 1))
            qs = _stack_heads(q_ref[0, pl.ds(u0, DIL_QB), sl])
            s = _dot_t(qs, k_ref[0, pl.ds(ks, wn), sl])
            s = s + jnp.concatenate([slopes[2 * p] * dist, slopes[2 * p + 1] * dist], axis=0)
            m = jnp.max(s, axis=1, keepdims=True)
            e = jnp.exp(s - m)
            l = jnp.sum(e, axis=1, keepdims=True)
            o = jnp.dot(e.astype(BF16), v_ref[0, pl.ds(ks, wn), sl],
                        preferred_element_type=F32) / l
            lse = jnp.broadcast_to(m + jnp.log(l), (2 * DIL_QB, LANES))
            o_ref[0, pl.ds(u0, DIL_QB), sl] = jnp.where(lo, o[:DIL_QB], o[DIL_QB:])
            lse_tile = jnp.where(lane == 2 * p, lse[:DIL_QB], lse_tile)
            lse_tile = jnp.where(lane == 2 * p + 1, lse[DIL_QB:], lse_tile)
        lse_ref[0, pl.ds(u0, DIL_QB), :] = lse_tile
        return carry

    lax.fori_loop(0, L // DIL_QB, body, 0)


def _dil_branch(q, k, v, B, S, dilation):
    L = S // dilation
    wn = _dil_window(L)
    spec = pl.BlockSpec((1, L, D_GRP), lambda b, rho: (b, 0, rho))
    dist = _dil_dist_table(L, dilation)
    return pl.pallas_call(
        functools.partial(_dil_kernel, L=L, slopes=_alibi_slopes(N_HEADS)),
        grid=(B, dilation),
        in_specs=[spec, spec, spec,
                  pl.BlockSpec((L // DIL_QB, DIL_QB, wn), lambda b, rho: (0, 0, 0))],
        out_specs=[spec, pl.BlockSpec((1, L, LANES), lambda b, rho: (b, 0, rho))],
        out_shape=[jax.ShapeDtypeStruct((B, L, dilation * D_GRP), F32),
                   jax.ShapeDtypeStruct((B, L, dilation * LANES), F32)],
        compiler_params=_cparams("parallel", "arbitrary"),
        name=f"dil{dilation}",
    )(q, k, v, dist)


def _mid_kernel(x_ref, ona_ref, o1_ref, o2_ref, o3_ref, l1_ref, l2_ref, l3_ref, mod_ref,
                gna_ref, gdil_ref, wout_ref, g2_ref, wqt_ref, sk_ref,
                x1_ref, h2t_ref, sct_ref, *stage_refs):
    tm = x_ref.shape[0]

    def token_order(ref, d, stage_ref):
        w = ref.shape[2] // d
        if d == 1:
            return [ref[0, :, c * LANES:(c + 1) * LANES] for c in range(w // LANES)]
        for rho in range(d):
            for c in range(w // LANES):
                col = rho * w + c * LANES
                stage_ref[c, pl.ds(rho, tm // d, stride=d), :] = ref[0, :, col:col + LANES]
        return [stage_ref[c] for c in range(w // LANES)]

    o_stage, l_stage = stage_refs[:len(DILATIONS)], stage_refs[len(DILATIONS):]
    o1, o2, o3 = (token_order(r, d, s) for r, d, s in zip((o1_ref, o2_ref, o3_ref), DILATIONS, o_stage))
    l1, l2, l3 = (token_order(r, d, s)[0] for r, d, s in zip((l1_ref, l2_ref, l3_ref), DILATIONS, l_stage))
    mx = jnp.maximum(jnp.maximum(l1, l2), l3)
    e1, e2, e3 = jnp.exp(l1 - mx), jnp.exp(l2 - mx), jnp.exp(l3 - mx)
    inv = 1.0 / (e1 + e2 + e3)
    alphas = (e1 * inv, e2 * inv, e3 * inv)
    lo = lax.broadcasted_iota(jnp.int32, (tm, LANES), 1) < HEAD_DIM
    parts = []
    for p in range(N_HEADS // 2):
        sl = slice(LANES * p, LANES * (p + 1))
        acc = jnp.zeros((tm, LANES), F32)
        for a, o in zip(alphas, (o1, o2, o3)):
            w = jnp.where(lo, a[:, 2 * p:2 * p + 1], a[:, 2 * p + 1:2 * p + 2])
            acc = acc + w * o[p]
        parts.append(acc)
    o_dil = jnp.concatenate(parts, axis=1)

    na_n = _rms(ona_ref[...].astype(F32), gna_ref[...]).astype(BF16)
    dil_n = _rms(o_dil, gdil_ref[...]).astype(BF16)
    y = jnp.dot(na_n, wout_ref[0:D_GRP, :], preferred_element_type=F32)
    y = y + jnp.dot(dil_n, wout_ref[D_GRP:2 * D_GRP, :], preferred_element_type=F32)
    g1 = mod_ref[0, 2:3, :]
    sh2 = mod_ref[0, 3:4, :]
    sc2 = mod_ref[0, 4:5, :]
    x1 = x_ref[...] + g1 * y
    x1_ref[...] = x1
    h2 = _rms(x1, g2_ref[...]) * (1.0 + sc2) + sh2
    h2t = h2.T.astype(BF16)
    h2t_ref[...] = h2t
    qpt = jnp.dot(wqt_ref[...], h2t, preferred_element_type=F32).astype(BF16)
    for hp in range(2 * PEER_HEADS):
        rows = slice(NKEYS * hp, NKEYS * (hp + 1))
        sct_ref[rows, :] = jnp.dot(sk_ref[hp % 2], qpt[rows, :], preferred_element_type=F32)


def _mid(x2, o_na, dil_outs, mod3, g_na, g_dil, w_out_b, g2, wq_t, subk, S):
    T, D = x2.shape
    tm = 512
    per_b = S // tm
    nq = wq_t.shape[0]
    row = lambda w: pl.BlockSpec((tm, w), lambda i: (i, 0))
    full = lambda s: pl.BlockSpec(s, lambda i: (0,) * len(s))
    view = lambda d, w: pl.BlockSpec((1, tm // d, d * w), lambda i: (i // per_b, i % per_b, 0))
    (o1, l1), (o2, l2), (o3, l3) = dil_outs
    return pl.pallas_call(
        _mid_kernel,
        grid=(T // tm,),
        in_specs=[row(D), row(D_GRP)] + [view(d, D_GRP) for d in DILATIONS]
                 + [view(d, LANES) for d in DILATIONS]
                 + [pl.BlockSpec((1, 6, D), lambda i: (i // per_b, 0, 0)),
                  full((1, D_GRP)), full((1, D_GRP)), full((D, D)), full((1, D)),
                  full((nq, D)), full((2, NKEYS, NKEYS))],
        out_specs=[row(D),
                   pl.BlockSpec((D, tm), lambda i: (0, i)),
                   pl.BlockSpec((nq, tm), lambda i: (0, i))],
        out_shape=[jax.ShapeDtypeStruct((T, D), F32),
                   jax.ShapeDtypeStruct((D, T), BF16),
                   jax.ShapeDtypeStruct((nq, T), F32)],
        scratch_shapes=[pltpu.VMEM((D_GRP // LANES, tm, LANES), F32)] * len(DILATIONS)
                       + [pltpu.VMEM((1, tm, LANES), F32)] * len(DILATIONS),
        compiler_params=_cparams("parallel"),
        name="mid",
    )(x2, o_na, o1, o2, o3, l1, l2, l3, mod3, g_na, g_dil, w_out_b, g2, wq_t, subk)


SUBLANES = 8
SEL_LANES = 128
NOT_TOP = float(TOPK)


def _merge_exchange_network(n):
    pairs = []
    t = (n - 1).bit_length()
    p = 1 << (t - 1)
    while p > 0:
        q, r, d = 1 << (t - 1), 0, p
        while d > 0:
            pairs += [(i, i + d) for i in range(n - d) if (i & p) == r]
            d, q, r = q - p, q >> 1, p
        p >>= 1
    return tuple(pairs)


def _bitonic_merge_network(n):
    pairs = []
    d = n // 2
    while d > 0:
        pairs += [(i, i + d) for i in range(n) if (i & d) == 0]
        d //= 2
    return tuple(pairs)


_SORT16 = _merge_exchange_network(TOPK)
_BITONIC16 = _bitonic_merge_network(TOPK)


def _compare_exchange(v, network):
    for i, j in network:
        v[i], v[j] = jnp.maximum(v[i], v[j]), jnp.minimum(v[i], v[j])


def _top16_values(v):
    v = list(v)
    _compare_exchange(v, _SORT16)
    for shift in (4, 2, 1):
        v = [jnp.maximum(v[k], pltpu.roll(v[TOPK - 1 - k], shift, axis=0)) for k in range(TOPK)]
        _compare_exchange(v, _BITONIC16)
    return v


def _all_sublanes_sum(x):
    for shift in (4, 2, 1):
        x = x + pltpu.roll(x, shift, axis=0)
    return x


def _select_tile_fast(s1, s2):
    n = s1[0].shape[1]
    sub = lax.broadcasted_iota(jnp.int32, (SUBLANES, n), 0)
    a = _top16_values(s1)
    b = _top16_values(s2)

    def spread(vals):
        out = vals[0]
        for k in range(1, SUBLANES):
            out = jnp.where(sub == k, vals[k], out)
        return out

    a_lo, a_hi = spread(a[:SUBLANES]), spread(a[SUBLANES:])
    b_lo, b_hi = spread(b[:SUBLANES]), spread(b[SUBLANES:])
    a_mid = jnp.where(sub < 4, -jnp.inf, a_lo)
    cand = []
    for r in range(4):
        cand += [a[r] + b_lo, a[r] + b_hi]
    cand += [a_mid + b[0], a_hi + b[0], a_mid + b[1], a_mid + b[2]]
    pad = [jnp.full((SUBLANES, n), -jnp.inf, F32)] * (TOPK - len(cand))
    thr = _top16_values(cand + pad)[TOPK - 1]
    m0 = a[0] + b[0]
    picked = [jnp.where(cv >= thr, 1.0, 0.0) for cv in cand]
    z = picked[0] * jnp.exp(cand[0] - m0)
    n_picked = picked[0]
    for pk, cv in zip(picked[1:], cand[1:]):
        z = z + pk * jnp.exp(cv - m0)
        n_picked = n_picked + pk
    inv_z = 1.0 / _all_sublanes_sum(z)
    counts = [_all_sublanes_sum(picked[2 * r] + picked[2 * r + 1]) for r in range(4)]
    n_mid = picked[8] + picked[10] + picked[11]
    counts += [_all_sublanes_sum(jnp.where(sub == r, n_mid, 0.0)) for r in range(4, 8)]
    counts += [_all_sublanes_sum(jnp.where(sub == r - 8, picked[9], 0.0)) for r in range(8, 16)]

    flag = jnp.where(_all_sublanes_sum(n_picked) != float(TOPK), 1.0, 0.0)
    for vals, keys, edge_used in ((a, s1, counts[TOPK - 1] > 0.0), (b, s2, counts[0] >= float(TOPK))):
        n_top = jnp.where(keys[0] >= vals[TOPK - 1], 1.0, 0.0)
        for x in keys[1:]:
            n_top = n_top + jnp.where(x >= vals[TOPK - 1], 1.0, 0.0)
        edge_tie = _all_sublanes_sum(n_top) != float(TOPK)
        flag = jnp.where(edge_tie & edge_used, 1.0, flag)

    count, c = [], []
    for x in s1:
        cnt = jnp.zeros(x.shape, F32)
        for r in range(TOPK):
            cnt = jnp.where(x == a[r], counts[r], cnt)
        count.append(cnt)
        c.append(jnp.exp(x - a[0]) * inv_z)
    rank2, f = [], []
    for x in s2:
        rk = jnp.full(x.shape, NOT_TOP, F32)
        for cc in range(TOPK - 1, -1, -1):
            rk = jnp.where(b[cc] <= x, float(cc), rk)
        rank2.append(rk)
        f.append(jnp.exp(x - b[0]))
    return rank2, f, count, c, flag


def _top16_exact(s, iota):
    work = s
    rank = jnp.full(s.shape, NOT_TOP, F32)
    vals = []
    for r in range(TOPK):
        m = jnp.max(work, axis=0, keepdims=True)
        first = jnp.min(jnp.where(work == m, iota, float(s.shape[0])), axis=0, keepdims=True)
        sel = iota == first
        rank = jnp.where(sel, float(r), rank)
        work = jnp.where(sel, -jnp.inf, work)
        vals.append(m)
    return vals, rank


def _select_tile_exact(s1, s2):
    n = s1.shape[1]
    iota_k = lax.broadcasted_iota(jnp.int32, (NKEYS, n), 0).astype(F32)
    iota_c = lax.broadcasted_iota(jnp.int32, (TOPK * TOPK, n), 0).astype(F32)
    iota_t = lax.broadcasted_iota(jnp.int32, (TOPK, n), 0)
    a, rank1 = _top16_exact(s1, iota_k)
    b, rank2 = _top16_exact(s2, iota_k)
    b16 = jnp.zeros((TOPK, n), F32)
    for cc in range(TOPK):
        b16 = jnp.where(iota_t == cc, b[cc], b16)
    cand = jnp.concatenate([a[r] + b16 for r in range(TOPK)], axis=0)
    work = cand
    picked = jnp.zeros(cand.shape, F32)
    for _ in range(TOPK):
        m = jnp.max(work, axis=0, keepdims=True)
        first = jnp.min(jnp.where(work == m, iota_c, float(TOPK * TOPK)), axis=0, keepdims=True)
        sel = iota_c == first
        picked = jnp.where(sel, 1.0, picked)
        work = jnp.where(sel, -jnp.inf, work)
    z = jnp.sum(picked * jnp.exp(cand - (a[0] + b[0])), axis=0, keepdims=True)
    count = jnp.zeros((NKEYS, n), F32)
    for r in range(TOPK):
        n_r = jnp.sum(picked[r * TOPK:(r + 1) * TOPK], axis=0, keepdims=True)
        count = jnp.where(rank1 == float(r), n_r, count)
    return rank2, jnp.exp(s2 - b[0]), count, jnp.exp(s1 - a[0]) / z


def _select_kernel(sc_ref, r2_ref, f_ref, cnt_ref, c_ref):
    tb = sc_ref.shape[1]
    vregs = NKEYS // SUBLANES
    flags = []
    for lt in range(tb // SEL_LANES):
        lanes = slice(lt * SEL_LANES, (lt + 1) * SEL_LANES)

        def fast(h, flag, lanes=lanes):
            base = pl.multiple_of(h * 2 * NKEYS, 2 * NKEYS)
            out = pl.multiple_of(h * NKEYS, NKEYS)
            s1 = [sc_ref[pl.ds(base + SUBLANES * k, SUBLANES), lanes] for k in range(vregs)]
            s2 = [sc_ref[pl.ds(base + NKEYS + SUBLANES * k, SUBLANES), lanes] for k in range(vregs)]
            rank2, f, count, c, bad = _select_tile_fast(s1, s2)
            for k in range(0, vregs, 2):
                rows = pl.ds(out + SUBLANES * k, 2 * SUBLANES)
                r2_ref[rows, lanes] = jnp.concatenate(rank2[k:k + 2], axis=0).astype(BF16)
                f_ref[rows, lanes] = jnp.concatenate(f[k:k + 2], axis=0).astype(BF16)
            for k in range(vregs):
                rows = pl.ds(out + SUBLANES * k, SUBLANES)
                cnt_ref[rows, lanes] = count[k]
                c_ref[rows, lanes] = c[k]
            return jnp.maximum(flag, bad)

        flags.append(lax.fori_loop(0, PEER_HEADS, fast, jnp.zeros((SUBLANES, SEL_LANES), F32)))

    for lt in range(tb // SEL_LANES):
        lanes = slice(lt * SEL_LANES, (lt + 1) * SEL_LANES)

        @pl.when(jnp.max(flags[lt]) > 0.0)
        def _(lanes=lanes):

            def exact(h, carry, lanes=lanes):
                base = pl.multiple_of(h * 2 * NKEYS, 2 * NKEYS)
                rows = pl.ds(pl.multiple_of(h * NKEYS, NKEYS), NKEYS)
                rank2, f, count, c = _select_tile_exact(sc_ref[pl.ds(base, NKEYS), lanes],
                                                        sc_ref[pl.ds(base + NKEYS, NKEYS), lanes])
                r2_ref[rows, lanes] = rank2.astype(BF16)
                f_ref[rows, lanes] = f.astype(BF16)
                cnt_ref[rows, lanes] = count
                c_ref[rows, lanes] = c
                return carry

            lax.fori_loop(0, PEER_HEADS, exact, 0)


def _select(sct):
    nq, T = sct.shape
    tb = 512
    rows = PEER_HEADS * NKEYS
    spec = pl.BlockSpec((rows, tb), lambda i: (0, i))
    return pl.pallas_call(
        _select_kernel,
        grid=(T // tb,),
        in_specs=[pl.BlockSpec((nq, tb), lambda i: (0, i))],
        out_specs=[spec] * 4,
        out_shape=[jax.ShapeDtypeStruct((rows, T), BF16)] * 2 + [jax.ShapeDtypeStruct((rows, T), F32)] * 2,
        compiler_params=_cparams("parallel"),
        name="select",
    )(sct)


PEER_ROWS_PER_STEP = 16
PEER_SUB_ROWS = 4
PEER_LOOKAHEAD = 3
BF16_ROWS = 16
PEER_GATE_LANES = 256


def _peer_kernel(h2t_ref, u_ref, vt_ref, r2_ref, f_ref, cnt_ref, c_ref, x1_ref, mod_ref, fg_ref,
                 o_ref, acc_ref, a_ref, w_ref):
    n = pl.program_id(1)
    tb = h2t_ref.shape[1]

    @pl.when(n == 0)
    def _():
        acc_ref[...] = jnp.zeros_like(acc_ref)

    sub = PEER_SUB_ROWS * NKEYS
    n_sub = PEER_ROWS_PER_STEP // PEER_SUB_ROWS
    chunk = lambda s: slice(s * sub, (s + 1) * sub)

    def scores(s):
        a_ref[chunk(s), :] = jnp.dot(u_ref[chunk(s), :], h2t_ref[...],
                                     preferred_element_type=F32)

    def gate(s):
        for k in range(s * PEER_SUB_ROWS, (s + 1) * PEER_SUB_ROWS):
            for lt in range(tb // PEER_GATE_LANES):
                lanes = slice(lt * PEER_GATE_LANES, (lt + 1) * PEER_GATE_LANES)
                bcast = lambda ref, h, scale: jnp.broadcast_to(
                    ref[h, k:k + 1, lanes] * scale, (BF16_ROWS, PEER_GATE_LANES)).astype(BF16)
                cnt = [bcast(cnt_ref, h, 1.0) for h in range(PEER_HEADS)]
                cc = [bcast(c_ref, h, 0.5) for h in range(PEER_HEADS)]
                for q in range(NKEYS // BF16_ROWS):
                    rows = slice(k * NKEYS + q * BF16_ROWS, k * NKEYS + (q + 1) * BF16_ROWS)
                    g = jnp.zeros((BF16_ROWS, PEER_GATE_LANES), BF16)
                    for h in range(PEER_HEADS):
                        keys = slice(h * NKEYS + q * BF16_ROWS, h * NKEYS + (q + 1) * BF16_ROWS)
                        g = g + jnp.where(r2_ref[keys, lanes] < cnt[h], f_ref[keys, lanes], 0.0) * cc[h]
                    av = a_ref[rows, lanes]
                    twice_gelu = av * (1.0 + lax.erf(av * (2.0 ** -0.5)))
                    w_ref[rows, lanes] = g * twice_gelu.astype(BF16)

    def combine(s):
        acc_ref[...] += jnp.dot(vt_ref[:, chunk(s)], w_ref[chunk(s), :],
                                preferred_element_type=F32)

    for s in range(PEER_LOOKAHEAD):
        scores(s)
    for s in range(n_sub):
        if s + PEER_LOOKAHEAD < n_sub:
            scores(s + PEER_LOOKAHEAD)
        gate(s)
        combine(s)

    @pl.when(n == pl.num_programs(1) - 1)
    def _():
        g2 = mod_ref[0, 5:6, :]
        x2 = x1_ref[...] + g2 * acc_ref[...].T
        o_ref[...] = _rms(x2, fg_ref[...])


def _peer(h2t, u_b, vt_b, sel, x1, mod3, fg, S):
    D, T = h2t.shape
    tb = 512
    ic = PEER_ROWS_PER_STEP
    nc = ic * NKEYS
    per_b = S // tb
    rows = PEER_HEADS * NKEYS
    r2, f, cnt, c = sel
    tok = lambda r: pl.BlockSpec((r, tb), lambda t, n: (0, t))
    per_row = pl.BlockSpec((PEER_HEADS, ic, tb), lambda t, n: (0, n, t))
    return pl.pallas_call(
        _peer_kernel,
        grid=(T // tb, NKEYS // ic),
        in_specs=[tok(D),
                  pl.BlockSpec((nc, D), lambda t, n: (n, 0)),
                  pl.BlockSpec((D, nc), lambda t, n: (0, n)),
                  tok(rows), tok(rows), per_row, per_row,
                  pl.BlockSpec((tb, D), lambda t, n: (t, 0)),
                  pl.BlockSpec((1, 6, D), lambda t, n: (t // per_b, 0, 0)),
                  pl.BlockSpec((1, D), lambda t, n: (0, 0))],
        out_specs=pl.BlockSpec((tb, D), lambda t, n: (t, 0)),
        out_shape=jax.ShapeDtypeStruct((T, D), F32),
        scratch_shapes=[pltpu.VMEM((D, tb), F32), pltpu.VMEM((nc, tb), F32),
                        pltpu.VMEM((nc, tb), BF16)],
        compiler_params=_cparams("parallel", "arbitrary"),
        name="peer",
    )(h2t, u_b, vt_b, r2, f, cnt.reshape(PEER_HEADS, NKEYS, T), c.reshape(PEER_HEADS, NKEYS, T),
      x1, mod3, fg)


def kernel(x, c, ada_w, ada_b, norm1_g, w_in, na_rpb, out_norm_na_g, out_norm_dil_g, w_out,
           norm2_g, peer_wq, peer_subkeys, peer_u, peer_v, final_g):
    B, S, D = x.shape
    assert ada_w.shape[0] == 1, "single-layer block"
    T = B * S
    x2 = x.reshape(T, D)

    mod3 = _modulation(c, ada_w[0], ada_b[0]).reshape(B, 6, D)
    qkv = _qkv(x2, mod3, norm1_g[0].reshape(1, D), w_in[0].astype(BF16), B, S)
    qa, ka, va = qkv[:3]

    o_na = _na_attention(qa, ka, va, _na_bias_table(na_rpb[0], S), B, S).reshape(T, D_GRP)
    dil_outs = [_dil_branch(*qkv[3 + 3 * n:6 + 3 * n], B, S, d) for n, d in enumerate(DILATIONS)]

    x1, h2t, sct = _mid(x2, o_na, dil_outs, mod3,
                        out_norm_na_g[0].reshape(1, D_GRP), out_norm_dil_g[0].reshape(1, D_GRP),
                        w_out[0].astype(BF16), norm2_g[0].reshape(1, D),
                        peer_wq[0].T.astype(BF16), peer_subkeys[0].astype(BF16), S)
    sel = _select(sct)
    out = _peer(h2t, peer_u[0].astype(BF16), peer_v[0].T.astype(BF16), sel, x1, mod3,
                final_g.reshape(1, D), S)
    return out.reshape(B, S, D)
```

```python
import functools

import numpy as np
import jax
import jax.numpy as jnp
from jax import lax
from jax.experimental import pallas as pl
from jax.experimental.pallas import tpu as pltpu

F32 = jnp.float32
BF16 = jnp.bfloat16

D_MODEL = 1024
HEAD_DIM = 64
N_HEADS = 8
D_GRP = N_HEADS * HEAD_DIM
GRID_W = 64
NA_ROWS = 8
NA_COLS = 16
DIL_CONFIGS = ((128, 1), (512, 4), (2048, 16))
PEER_HEADS = 8
NKEYS = 128
TOPK = 16
EPS = 1e-6
MASKED = -1e30
LANES = 128
VMEM_LIMIT = 56 * 1024 * 1024

NA_QROWS = 4
NA_KROWS = 12
NA_QB = NA_QROWS * GRID_W
NA_KB = NA_KROWS * GRID_W
DIL_QB = 128
DIL_HALF = 64
DIL_GROUP = 4


def _cparams(*sem):
    return pltpu.CompilerParams(dimension_semantics=sem, vmem_limit_bytes=VMEM_LIMIT)


def _rms(x, g):
    ms = jnp.mean(x * x, axis=-1, keepdims=True)
    return x * lax.rsqrt(ms + EPS) * g


def _stack_heads(q):
    qf = q.astype(F32)
    lo = lax.broadcasted_iota(jnp.int32, qf.shape, 1) < HEAD_DIM
    return jnp.concatenate([jnp.where(lo, qf, 0.0), jnp.where(lo, 0.0, qf)], axis=0).astype(BF16)


def _dot_t(a, b):
    return lax.dot_general(a, b, (((1,), (1,)), ((), ())), preferred_element_type=F32)


def _mod_kernel(c_ref, w_ref, b_ref, o_ref):
    c = c_ref[...]
    s = c / (1.0 + jnp.exp(-c))
    o_ref[...] = jnp.dot(s, w_ref[...], preferred_element_type=F32,
                         precision=lax.Precision.HIGHEST) + b_ref[...]


def _modulation(c, ada_w, ada_b):
    B, D = c.shape
    n = ada_w.shape[1] // D
    return pl.pallas_call(
        _mod_kernel,
        grid=(n,),
        in_specs=[pl.BlockSpec((B, D), lambda j: (0, 0)),
                  pl.BlockSpec((D, D), lambda j: (0, j)),
                  pl.BlockSpec((1, D), lambda j: (0, j))],
        out_specs=pl.BlockSpec((B, D), lambda j: (0, j)),
        out_shape=jax.ShapeDtypeStruct((B, n * D), F32),
        compiler_params=_cparams("arbitrary"),
        name="mod",
    )(c, ada_w, ada_b.reshape(1, -1))


DILATIONS = tuple(d for _, d in DIL_CONFIGS)


def _qkv_kernel(x_ref, mod_ref, g_ref, w_ref, *refs):
    out_refs, stage_ref = refs[:-1], refs[-1]
    tm = x_ref.shape[0]
    sh1 = mod_ref[0, 0:1, :]
    sc1 = mod_ref[0, 1:2, :]
    h = _rms(x_ref[...], g_ref[...]) * (1.0 + sc1) + sh1
    hb = h.astype(BF16)
    for j in range(6):
        p = jnp.dot(hb, w_ref[:, j * D_GRP:(j + 1) * D_GRP], preferred_element_type=F32)
        if j % 3 == 0:
            p = p * (HEAD_DIM ** -0.5)
        if j < 3:
            out_refs[j][...] = p.astype(BF16)
            continue
        for c in range(D_GRP // LANES):
            stage_ref[c] = p[:, c * LANES:(c + 1) * LANES]
        for n, d in enumerate(DILATIONS):
            o_ref = out_refs[3 * n + j]
            for rho in range(d):
                for c in range(D_GRP // LANES):
                    col = rho * D_GRP + c * LANES
                    o_ref[0, :, col:col + LANES] = (
                        stage_ref[c, pl.ds(rho, tm // d, stride=d), :].astype(BF16))


def _qkv(x2, mod3, g, w_in_b, B, S):
    T, D = x2.shape
    tm = 512
    per_b = S // tm
    dil_specs, dil_shapes = [], []
    for d in DILATIONS:
        dil_specs += [pl.BlockSpec((1, tm // d, d * D_GRP), lambda i: (i // per_b, i % per_b, 0))] * 3
        dil_shapes += [jax.ShapeDtypeStruct((B, S // d, d * D_GRP), BF16)] * 3
    return pl.pallas_call(
        _qkv_kernel,
        grid=(T // tm,),
        in_specs=[pl.BlockSpec((tm, D), lambda i: (i, 0)),
                  pl.BlockSpec((1, 6, D), lambda i: (i // per_b, 0, 0)),
                  pl.BlockSpec((1, D), lambda i: (0, 0)),
                  pl.BlockSpec((D, 6 * D_GRP), lambda i: (0, 0))],
        out_specs=[pl.BlockSpec((tm, D_GRP), lambda i: (i, 0))] * 3 + dil_specs,
        out_shape=[jax.ShapeDtypeStruct((T, D_GRP), BF16)] * 3 + dil_shapes,
        scratch_shapes=[pltpu.VMEM((D_GRP // LANES, tm, LANES), F32)],
        compiler_params=_cparams("parallel"),
        name="qkv",
    )(x2, mod3, g, w_in_b)


def _na_bias_table(rpb, S):
    rows = S // GRID_W
    n_dr, n_dc = 2 * NA_ROWS - 1, 2 * NA_COLS - 1
    c = np.arange(GRID_W)[:, None]
    kc = np.arange(GRID_W)[None, :]
    c0 = np.clip(c - NA_COLS // 2, 0, GRID_W - NA_COLS)
    col_ok = (kc >= c0) & (kc < c0 + NA_COLS)
    dc = np.clip(kc - c + NA_COLS - 1, 0, n_dc - 1)
    onehot = ((dc[None] == np.arange(n_dc)[:, None, None]) & col_ok[None]).astype(np.float32)
    blocks = jnp.einsum('hrd,dck->hrck', rpb.astype(F32), jnp.asarray(onehot),
                        precision=lax.Precision.HIGHEST)
    blocks = jnp.where(jnp.asarray(col_ok)[None, None], blocks, MASKED)
    masked_block = jnp.full((rpb.shape[0], 1, GRID_W, GRID_W), MASKED, F32)
    blocks = jnp.concatenate([blocks, masked_block], axis=1)
    a = np.arange(NA_QROWS)[:, None]
    wr = np.arange(NA_KROWS)[None, :]
    idx = []
    for R in (0, 1, rows // NA_QROWS - 1):
        r = NA_QROWS * R + a
        r0 = np.clip(r - NA_ROWS // 2, 0, rows - NA_ROWS)
        w0 = int(np.clip(NA_QROWS * R - NA_ROWS // 2, 0, rows - NA_KROWS))
        krow = w0 + wr
        row_ok = (krow >= r0) & (krow < r0 + NA_ROWS)
        idx.append(np.where(row_ok, krow - r + NA_ROWS - 1, n_dr))
    t = jnp.take(blocks, jnp.asarray(np.stack(idx).reshape(-1), jnp.int32), axis=1)
    t = t.reshape(-1, 3, NA_QROWS, NA_KROWS, GRID_W, GRID_W)
    return t.transpose(1, 0, 2, 4, 3, 5).reshape(3, -1, NA_QB, NA_KB)


def _na_kernel(q_ref, k0_ref, k1_ref, k2_ref, v0_ref, v1_ref, v2_ref, bias_ref, o_ref):
    k_refs = (k0_ref, k1_ref, k2_ref)
    v_refs = (v0_ref, v1_ref, v2_ref)
    lo = lax.broadcasted_iota(jnp.int32, (NA_QB, LANES), 1) < HEAD_DIM
    kb = NA_KB // 3
    for p in range(N_HEADS // 2):
        sl = slice(LANES * p, LANES * (p + 1))
        qs = _stack_heads(q_ref[0, :, sl])
        s = jnp.concatenate([_dot_t(qs, kr[0, :, sl]) for kr in k_refs], axis=1)
        s = s + jnp.concatenate([bias_ref[0, 2 * p], bias_ref[0, 2 * p + 1]], axis=0)
        m = jnp.max(s, axis=1, keepdims=True)
        e = jnp.exp(s - m)
        l = jnp.sum(e, axis=1, keepdims=True)
        eb = e.astype(BF16)
        o = jnp.dot(eb[:, 0:kb], v_refs[0][0, :, sl], preferred_element_type=F32)
        for j in (1, 2):
            o = o + jnp.dot(eb[:, j * kb:(j + 1) * kb], v_refs[j][0, :, sl],
                            preferred_element_type=F32)
        o = o / l
        o_ref[0, :, sl] = jnp.where(lo, o[:NA_QB], o[NA_QB:]).astype(BF16)


def _na_attention(q, k, v, bias, B, S):
    nblk = S // NA_QB
    kblk = NA_KB // 3
    assert kblk == NA_QB
    q3, k3, v3 = (t.reshape(B, S, D_GRP) for t in (q, k, v))

    def w0(R):
        return jnp.clip(R - 1, 0, nblk - 3)

    def kv_spec(j):
        return pl.BlockSpec((1, kblk, D_GRP), lambda R, b: (b, w0(R) + j, 0))

    return pl.pallas_call(
        _na_kernel,
        grid=(nblk, B),
        in_specs=[pl.BlockSpec((1, NA_QB, D_GRP), lambda R, b: (b, R, 0)),
                  kv_spec(0), kv_spec(1), kv_spec(2), kv_spec(0), kv_spec(1), kv_spec(2),
                  pl.BlockSpec((1, N_HEADS, NA_QB, NA_KB),
                               lambda R, b: (jnp.minimum(R, 1) + R // (nblk - 1), 0, 0, 0))],
        out_specs=pl.BlockSpec((1, NA_QB, D_GRP), lambda R, b: (b, R, 0)),
        out_shape=jax.ShapeDtypeStruct((B, S, D_GRP), BF16),
        compiler_params=_cparams("arbitrary", "arbitrary"),
        name="na",
    )(q3, k3, k3, k3, v3, v3, v3, bias)


def _alibi_slopes(n):
    return [float(2.0 ** (-8.0 * (i + 1) / n)) for i in range(n)]


def _dil_window(L):
    return min(L, DIL_QB + 2 * DIL_HALF)


def _dil_window_start(u0, L, clip):
    return clip(u0 - DIL_HALF, 0, L - _dil_window(L))


def _dil_dist_table(L, dilation):
    wn = _dil_window(L)
    nqb = L // DIL_QB
    tab = np.empty((nqb, DIL_QB, wn), np.float32)
    for qb in range(nqb):
        ks = int(_dil_window_start(qb * DIL_QB, L, np.clip))
        uq = qb * DIL_QB + np.arange(DIL_QB)[:, None]
        uk = ks + np.arange(wn)[None, :]
        delta = np.abs(uk - uq)
        tab[qb] = np.where(delta <= DIL_HALF, -float(dilation) * delta, MASKED)
    return jnp.asarray(tab)


def _dil_kernel(q_ref, k_ref, v_ref, d_ref, o_ref, lse_ref, *, L, group, slopes):
    wn = _dil_window(L)
    lane = lax.broadcasted_iota(jnp.int32, (DIL_QB, LANES), 1)
    lo = lane < HEAD_DIM

    def body(qb, carry):
        u0 = pl.multiple_of(qb * DIL_QB, DIL_QB)
        ks = pl.multiple_of(_dil_window_start(u0, L, jnp.clip), DIL_HALF)
        dist = d_ref[qb]
        for g in range(group):
            lse_tile = jnp.zeros((DIL_QB, LANES), F32)
            for p in range(N_HEADS // 2):
                sl = slice(g * D_GRP + LANES * p, g * D_GRP + LANES * (p + 1))
                qs = _stack_heads(q_ref[0, pl.ds(u0, DIL_QB), sl])
                s = _dot_t(qs, k_ref[0, pl.ds(ks, wn), sl])
                s = s + jnp.concatenate([slopes[2 * p] * dist, slopes[2 * p + 1] * dist], axis=0)
                m = jnp.max(s, axis=1, keepdims=True)
                e = jnp.exp(s - m)
                l = jnp.sum(e, axis=1, keepdims=True)
                o = jnp.dot(e.astype(BF16), v_ref[0, pl.ds(ks, wn), sl],
                            preferred_element_type=F32) / l
                lse = jnp.broadcast_to(m + jnp.log(l), (2 * DIL_QB, LANES))
                o_ref[0, pl.ds(u0, DIL_QB), sl] = jnp.where(lo, o[:DIL_QB], o[DIL_QB:])
                lse_tile = jnp.where(lane == 2 * p, lse[:DIL_QB], lse_tile)
                lse_tile = jnp.where(lane == 2 * p + 1, lse[DIL_QB:], lse_tile)
            lse_ref[0, pl.ds(u0, DIL_QB), g * LANES:(g + 1) * LANES] = lse_tile
        return carry

    lax.fori_loop(0, L // DIL_QB, body, 0)


def _dil_branch(q, k, v, B, S, dilation):
    L = S // dilation
    wn = _dil_window(L)
    group = min(dilation, DIL_GROUP)
    spec = pl.BlockSpec((1, L, group * D_GRP), lambda b, rho: (b, 0, rho))
    dist = _dil_dist_table(L, dilation)
    return pl.pallas_call(
        functools.partial(_dil_kernel, L=L, group=group, slopes=_alibi_slopes(N_HEADS)),
        grid=(B, dilation // group),
        in_specs=[spec, spec, spec,
                  pl.BlockSpec((L // DIL_QB, DIL_QB, wn), lambda b, rho: (0, 0, 0))],
        out_specs=[spec, pl.BlockSpec((1, L, group * LANES), lambda b, rho: (b, 0, rho))],
        out_shape=[jax.ShapeDtypeStruct((B, L, dilation * D_GRP), F32),
                   jax.ShapeDtypeStruct((B, L, dilation * LANES), F32)],
        compiler_params=_cparams("parallel", "arbitrary"),
        name=f"dil{dilation}",
    )(q, k, v, dist)


def _mid_kernel(x_ref, ona_ref, o1_ref, o2_ref, o3_ref, l1_ref, l2_ref, l3_ref, mod_ref,
                gna_ref, gdil_ref, wout_ref, g2_ref, wqt_ref, sk_ref,
                x1_ref, h2t_ref, sct_ref, *stage_refs):
    tm = x_ref.shape[0]

    def token_order(ref, d, stage_ref):
        w = ref.shape[2] // d
        if d == 1:
            return [ref[0, :, c * LANES:(c + 1) * LANES] for c in range(w // LANES)]
        for rho in range(d):
            for c in range(w // LANES):
                col = rho * w + c * LANES
                stage_ref[c, pl.ds(rho, tm // d, stride=d), :] = ref[0, :, col:col + LANES]
        return [stage_ref[c] for c in range(w // LANES)]

    o_stage, l_stage = stage_refs[:len(DILATIONS)], stage_refs[len(DILATIONS):]
    o1, o2, o3 = (token_order(r, d, s) for r, d, s in zip((o1_ref, o2_ref, o3_ref), DILATIONS, o_stage))
    l1, l2, l3 = (token_order(r, d, s)[0] for r, d, s in zip((l1_ref, l2_ref, l3_ref), DILATIONS, l_stage))
    mx = jnp.maximum(jnp.maximum(l1, l2), l3)
    e1, e2, e3 = jnp.exp(l1 - mx), jnp.exp(l2 - mx), jnp.exp(l3 - mx)
    inv = 1.0 / (e1 + e2 + e3)
    alphas = (e1 * inv, e2 * inv, e3 * inv)
    lo = lax.broadcasted_iota(jnp.int32, (tm, LANES), 1) < HEAD_DIM
    parts = []
    for p in range(N_HEADS // 2):
        sl = slice(LANES * p, LANES * (p + 1))
        acc = jnp.zeros((tm, LANES), F32)
        for a, o in zip(alphas, (o1, o2, o3)):
            w = jnp.where(lo, a[:, 2 * p:2 * p + 1], a[:, 2 * p + 1:2 * p + 2])
            acc = acc + w * o[p]
        parts.append(acc)
    o_dil = jnp.concatenate(parts, axis=1)

    na_n = _rms(ona_ref[...].astype(F32), gna_ref[...]).astype(BF16)
    dil_n = _rms(o_dil, gdil_ref[...]).astype(BF16)
    y = jnp.dot(na_n, wout_ref[0:D_GRP, :], preferred_element_type=F32)
    y = y + jnp.dot(dil_n, wout_ref[D_GRP:2 * D_GRP, :], preferred_element_type=F32)
    g1 = mod_ref[0, 2:3, :]
    sh2 = mod_ref[0, 3:4, :]
    sc2 = mod_ref[0, 4:5, :]
    x1 = x_ref[...] + g1 * y
    x1_ref[...] = x1
    h2 = _rms(x1, g2_ref[...]) * (1.0 + sc2) + sh2
    h2t = h2.T.astype(BF16)
    h2t_ref[...] = h2t
    qpt = jnp.dot(wqt_ref[...], h2t, preferred_element_type=F32).astype(BF16)
    for hp in range(2 * PEER_HEADS):
        rows = slice(NKEYS * hp, NKEYS * (hp + 1))
        sct_ref[rows, :] = jnp.dot(sk_ref[hp % 2], qpt[rows, :], preferred_element_type=F32)


def _mid(x2, o_na, dil_outs, mod3, g_na, g_dil, w_out_b, g2, wq_t, subk, S):
    T, D = x2.shape
    tm = 512
    per_b = S // tm
    nq = wq_t.shape[0]
    row = lambda w: pl.BlockSpec((tm, w), lambda i: (i, 0))
    full = lambda s: pl.BlockSpec(s, lambda i: (0,) * len(s))
    view = lambda d, w: pl.BlockSpec((1, tm // d, d * w), lambda i: (i // per_b, i % per_b, 0))
    (o1, l1), (o2, l2), (o3, l3) = dil_outs
    return pl.pallas_call(
        _mid_kernel,
        grid=(T // tm,),
        in_specs=[row(D), row(D_GRP)] + [view(d, D_GRP) for d in DILATIONS]
                 + [view(d, LANES) for d in DILATIONS]
                 + [pl.BlockSpec((1, 6, D), lambda i: (i // per_b, 0, 0)),
                  full((1, D_GRP)), full((1, D_GRP)), full((D, D)), full((1, D)),
                  full((nq, D)), full((2, NKEYS, NKEYS))],
        out_specs=[row(D),
                   pl.BlockSpec((D, tm), lambda i: (0, i)),
                   pl.BlockSpec((nq, tm), lambda i: (0, i))],
        out_shape=[jax.ShapeDtypeStruct((T, D), F32),
                   jax.ShapeDtypeStruct((D, T), BF16),
                   jax.ShapeDtypeStruct((nq, T), F32)],
        scratch_shapes=[pltpu.VMEM((D_GRP // LANES, tm, LANES), F32)] * len(DILATIONS)
                       + [pltpu.VMEM((1, tm, LANES), F32)] * len(DILATIONS),
        compiler_params=_cparams("parallel"),
        name="mid",
    )(x2, o_na, o1, o2, o3, l1, l2, l3, mod3, g_na, g_dil, w_out_b, g2, wq_t, subk)


SUBLANES = 8
SEL_LANES = 128
NOT_TOP = float(TOPK)


def _merge_exchange_network(n):
    pairs = []
    t = (n - 1).bit_length()
    p = 1 << (t - 1)
    while p > 0:
        q, r, d = 1 << (t - 1), 0, p
        while d > 0:
            pairs += [(i, i + d) for i in range(n - d) if (i & p) == r]
            d, q, r = q - p, q >> 1, p
        p >>= 1
    return tuple(pairs)


def _bitonic_merge_network(n):
    pairs = []
    d = n // 2
    while d > 0:
        pairs += [(i, i + d) for i in range(n) if (i & d) == 0]
        d //= 2
    return tuple(pairs)


_SORT16 = _merge_exchange_network(TOPK)
_BITONIC16 = _bitonic_merge_network(TOPK)


def _compare_exchange(v, network):
    for i, j in network:
        v[i], v[j] = jnp.maximum(v[i], v[j]), jnp.minimum(v[i], v[j])


def _top16_values(v):
    v = list(v)
    _compare_exchange(v, _SORT16)
    for shift in (4, 2, 1):
        v = [jnp.maximum(v[k], pltpu.roll(v[TOPK - 1 - k], shift, axis=0)) for k in range(TOPK)]
        _compare_exchange(v, _BITONIC16)
    return v


def _all_sublanes_sum(x):
    for shift in (4, 2, 1):
        x = x + pltpu.roll(x, shift, axis=0)
    return x


def _select_tile_fast(s1, s2):
    n = s1[0].shape[1]
    sub = lax.broadcasted_iota(jnp.int32, (SUBLANES, n), 0)
    a = _top16_values(s1)
    b = _top16_values(s2)

    def spread(vals):
        out = vals[0]
        for k in range(1, SUBLANES):
            out = jnp.where(sub == k, vals[k], out)
        return out

    a_lo, a_hi = spread(a[:SUBLANES]), spread(a[SUBLANES:])
    b_lo, b_hi = spread(b[:SUBLANES]), spread(b[SUBLANES:])
    a_mid = jnp.where(sub < 4, -jnp.inf, a_lo)
    cand = []
    for r in range(4):
        cand += [a[r] + b_lo, a[r] + b_hi]
    cand += [a_mid + b[0], a_hi + b[0], a_mid + b[1], a_mid + b[2]]
    pad = [jnp.full((SUBLANES, n), -jnp.inf, F32)] * (TOPK - len(cand))
    thr = _top16_values(cand + pad)[TOPK - 1]
    m0 = a[0] + b[0]
    picked = [jnp.where(cv >= thr, 1.0, 0.0) for cv in cand]
    z = picked[0] * jnp.exp(cand[0] - m0)
    n_picked = picked[0]
    for pk, cv in zip(picked[1:], cand[1:]):
        z = z + pk * jnp.exp(cv - m0)
        n_picked = n_picked + pk
    inv_z = 1.0 / _all_sublanes_sum(z)
    counts = [_all_sublanes_sum(picked[2 * r] + picked[2 * r + 1]) for r in range(4)]
    n_mid = picked[8] + picked[10] + picked[11]
    counts += [_all_sublanes_sum(jnp.where(sub == r, n_mid, 0.0)) for r in range(4, 8)]
    counts += [_all_sublanes_sum(jnp.where(sub == r - 8, picked[9], 0.0)) for r in range(8, 16)]

    flag = jnp.where(_all_sublanes_sum(n_picked) != float(TOPK), 1.0, 0.0)
    for vals, keys, edge_used in ((a, s1, counts[TOPK - 1] > 0.0), (b, s2, counts[0] >= float(TOPK))):
        n_top = jnp.where(keys[0] >= vals[TOPK - 1], 1.0, 0.0)
        for x in keys[1:]:
            n_top = n_top + jnp.where(x >= vals[TOPK - 1], 1.0, 0.0)
        edge_tie = _all_sublanes_sum(n_top) != float(TOPK)
        flag = jnp.where(edge_tie & edge_used, 1.0, flag)

    count, c = [], []
    for x in s1:
        cnt = jnp.zeros(x.shape, F32)
        for r in range(TOPK):
            cnt = jnp.where(x == a[r], counts[r], cnt)
        count.append(cnt)
        c.append(jnp.exp(x - a[0]) * inv_z)
    rank2, f = [], []
    for x in s2:
        rk = jnp.full(x.shape, NOT_TOP, F32)
        for cc in range(TOPK - 1, -1, -1):
            rk = jnp.where(b[cc] <= x, float(cc), rk)
        rank2.append(rk)
        f.append(jnp.exp(x - b[0]))
    return rank2, f, count, c, flag


def _top16_exact(s, iota):
    work = s
    rank = jnp.full(s.shape, NOT_TOP, F32)
    vals = []
    for r in range(TOPK):
        m = jnp.max(work, axis=0, keepdims=True)
        first = jnp.min(jnp.where(work == m, iota, float(s.shape[0])), axis=0, keepdims=True)
        sel = iota == first
        rank = jnp.where(sel, float(r), rank)
        work = jnp.where(sel, -jnp.inf, work)
        vals.append(m)
    return vals, rank


def _select_tile_exact(s1, s2):
    n = s1.shape[1]
    iota_k = lax.broadcasted_iota(jnp.int32, (NKEYS, n), 0).astype(F32)
    iota_c = lax.broadcasted_iota(jnp.int32, (TOPK * TOPK, n), 0).astype(F32)
    iota_t = lax.broadcasted_iota(jnp.int32, (TOPK, n), 0)
    a, rank1 = _top16_exact(s1, iota_k)
    b, rank2 = _top16_exact(s2, iota_k)
    b16 = jnp.zeros((TOPK, n), F32)
    for cc in range(TOPK):
        b16 = jnp.where(iota_t == cc, b[cc], b16)
    cand = jnp.concatenate([a[r] + b16 for r in range(TOPK)], axis=0)
    work = cand
    picked = jnp.zeros(cand.shape, F32)
    for _ in range(TOPK):
        m = jnp.max(work, axis=0, keepdims=True)
        first = jnp.min(jnp.where(work == m, iota_c, float(TOPK * TOPK)), axis=0, keepdims=True)
        sel = iota_c == first
        picked = jnp.where(sel, 1.0, picked)
        work = jnp.where(sel, -jnp.inf, work)
    z = jnp.sum(picked * jnp.exp(cand - (a[0] + b[0])), axis=0, keepdims=True)
    count = jnp.zeros((NKEYS, n), F32)
    for r in range(TOPK):
        n_r = jnp.sum(picked[r * TOPK:(r + 1) * TOPK], axis=0, keepdims=True)
        count = jnp.where(rank1 == float(r), n_r, count)
    return rank2, jnp.exp(s2 - b[0]), count, jnp.exp(s1 - a[0]) / z


def _select_kernel(sc_ref, r2_ref, f_ref, cnt_ref, c_ref):
    tb = sc_ref.shape[1]
    vregs = NKEYS // SUBLANES
    flags = []
    for lt in range(tb // SEL_LANES):
        lanes = slice(lt * SEL_LANES, (lt + 1) * SEL_LANES)

        def fast(h, flag, lanes=lanes):
            base = pl.multiple_of(h * 2 * NKEYS, 2 * NKEYS)
            out = pl.multiple_of(h * NKEYS, NKEYS)
            s1 = [sc_ref[pl.ds(base + SUBLANES * k, SUBLANES), lanes] for k in range(vregs)]
            s2 = [sc_ref[pl.ds(base + NKEYS + SUBLANES * k, SUBLANES), lanes] for k in range(vregs)]
            rank2, f, count, c, bad = _select_tile_fast(s1, s2)
            for k in range(0, vregs, 2):
                rows = pl.ds(out + SUBLANES * k, 2 * SUBLANES)
                r2_ref[rows, lanes] = jnp.concatenate(rank2[k:k + 2], axis=0).astype(BF16)
                f_ref[rows, lanes] = jnp.concatenate(f[k:k + 2], axis=0).astype(BF16)
            for k in range(vregs):
                rows = pl.ds(out + SUBLANES * k, SUBLANES)
                cnt_ref[rows, lanes] = count[k]
                c_ref[rows, lanes] = c[k]
            return jnp.maximum(flag, bad)

        flags.append(lax.fori_loop(0, PEER_HEADS, fast, jnp.zeros((SUBLANES, SEL_LANES), F32)))

    for lt in range(tb // SEL_LANES):
        lanes = slice(lt * SEL_LANES, (lt + 1) * SEL_LANES)

        @pl.when(jnp.max(flags[lt]) > 0.0)
        def _(lanes=lanes):

            def exact(h, carry, lanes=lanes):
                base = pl.multiple_of(h * 2 * NKEYS, 2 * NKEYS)
                rows = pl.ds(pl.multiple_of(h * NKEYS, NKEYS), NKEYS)
                rank2, f, count, c = _select_tile_exact(sc_ref[pl.ds(base, NKEYS), lanes],
                                                        sc_ref[pl.ds(base + NKEYS, NKEYS), lanes])
                r2_ref[rows, lanes] = rank2.astype(BF16)
                f_ref[rows, lanes] = f.astype(BF16)
                cnt_ref[rows, lanes] = count
                c_ref[rows, lanes] = c
                return carry

            lax.fori_loop(0, PEER_HEADS, exact, 0)


def _select(sct):
    nq, T = sct.shape
    tb = 512
    rows = PEER_HEADS * NKEYS
    spec = pl.BlockSpec((rows, tb), lambda i: (0, i))
    return pl.pallas_call(
        _select_kernel,
        grid=(T // tb,),
        in_specs=[pl.BlockSpec((nq, tb), lambda i: (0, i))],
        out_specs=[spec] * 4,
        out_shape=[jax.ShapeDtypeStruct((rows, T), BF16)] * 2 + [jax.ShapeDtypeStruct((rows, T), F32)] * 2,
        compiler_params=_cparams("parallel"),
        name="select",
    )(sct)


PEER_ROWS_PER_STEP = 16
PEER_SUB_ROWS = 4
PEER_LOOKAHEAD = 3
BF16_ROWS = 16
PEER_GATE_LANES = 256


def _peer_kernel(h2t_ref, u_ref, vt_ref, r2_ref, f_ref, cnt_ref, c_ref, x1_ref, mod_ref, fg_ref,
                 o_ref, acc_ref, a_ref, w_ref):
    n = pl.program_id(1)
    tb = h2t_ref.shape[1]

    @pl.when(n == 0)
    def _():
        acc_ref[...] = jnp.zeros_like(acc_ref)

    sub = PEER_SUB_ROWS * NKEYS
    n_sub = PEER_ROWS_PER_STEP // PEER_SUB_ROWS
    chunk = lambda s: slice(s * sub, (s + 1) * sub)

    def scores(s):
        a_ref[chunk(s), :] = jnp.dot(u_ref[chunk(s), :], h2t_ref[...],
                                     preferred_element_type=F32)

    def gate(s):
        for k in range(s * PEER_SUB_ROWS, (s + 1) * PEER_SUB_ROWS):
            for lt in range(tb // PEER_GATE_LANES):
                lanes = slice(lt * PEER_GATE_LANES, (lt + 1) * PEER_GATE_LANES)
                bcast = lambda ref, h, scale: jnp.broadcast_to(
                    ref[h, k:k + 1, lanes] * scale, (BF16_ROWS, PEER_GATE_LANES)).astype(BF16)
                cnt = [bcast(cnt_ref, h, 1.0) for h in range(PEER_HEADS)]
                cc = [bcast(c_ref, h, 0.5) for h in range(PEER_HEADS)]
                for q in range(NKEYS // BF16_ROWS):
                    rows = slice(k * NKEYS + q * BF16_ROWS, k * NKEYS + (q + 1) * BF16_ROWS)
                    g = None
                    for h in range(PEER_HEADS):
                        keys = slice(h * NKEYS + q * BF16_ROWS, h * NKEYS + (q + 1) * BF16_ROWS)
                        term = jnp.where(r2_ref[keys, lanes] < cnt[h], f_ref[keys, lanes], 0.0) * cc[h]
                        g = term if g is None else g + term
                    av = a_ref[rows, lanes]
                    twice_gelu = av * (1.0 + lax.erf(av * (2.0 ** -0.5)))
                    w_ref[rows, lanes] = g * twice_gelu.astype(BF16)

    def combine(s):
        acc_ref[...] += jnp.dot(vt_ref[:, chunk(s)], w_ref[chunk(s), :],
                                preferred_element_type=F32)

    for s in range(PEER_LOOKAHEAD):
        scores(s)
    for s in range(n_sub):
        if s + PEER_LOOKAHEAD < n_sub:
            scores(s + PEER_LOOKAHEAD)
        gate(s)
        combine(s)

    @pl.when(n == pl.num_programs(1) - 1)
    def _():
        g2 = mod_ref[0, 5:6, :]
        x2 = x1_ref[...] + g2 * acc_ref[...].T
        o_ref[...] = _rms(x2, fg_ref[...])


def _peer(h2t, u_b, vt_b, sel, x1, mod3, fg, S):
    D, T = h2t.shape
    tb = 512
    ic = PEER_ROWS_PER_STEP
    nc = ic * NKEYS
    per_b = S // tb
    rows = PEER_HEADS * NKEYS
    r2, f, cnt, c = sel
    tok = lambda r: pl.BlockSpec((r, tb), lambda t, n: (0, t))
    per_row = pl.BlockSpec((PEER_HEADS, ic, tb), lambda t, n: (0, n, t))
    return pl.pallas_call(
        _peer_kernel,
        grid=(T // tb, NKEYS // ic),
        in_specs=[tok(D),
                  pl.BlockSpec((nc, D), lambda t, n: (n, 0)),
                  pl.BlockSpec((D, nc), lambda t, n: (0, n)),
                  tok(rows), tok(rows), per_row, per_row,
                  pl.BlockSpec((tb, D), lambda t, n: (t, 0)),
                  pl.BlockSpec((1, 6, D), lambda t, n: (t // per_b, 0, 0)),
                  pl.BlockSpec((1, D), lambda t, n: (0, 0))],
        out_specs=pl.BlockSpec((tb, D), lambda t, n: (t, 0)),
        out_shape=jax.ShapeDtypeStruct((T, D), F32),
        scratch_shapes=[pltpu.VMEM((D, tb), F32), pltpu.VMEM((nc, tb), F32),
                        pltpu.VMEM((nc, tb), BF16)],
        compiler_params=_cparams("parallel", "arbitrary"),
        name="peer",
    )(h2t, u_b, vt_b, r2, f, cnt.reshape(PEER_HEADS, NKEYS, T), c.reshape(PEER_HEADS, NKEYS, T),
      x1, mod3, fg)


def kernel(x, c, ada_w, ada_b, norm1_g, w_in, na_rpb, out_norm_na_g, out_norm_dil_g, w_out,
           norm2_g, peer_wq, peer_subkeys, peer_u, peer_v, final_g):
    B, S, D = x.shape
    assert ada_w.shape[0] == 1, "single-layer block"
    T = B * S
    x2 = x.reshape(T, D)

    mod3 = _modulation(c, ada_w[0], ada_b[0]).reshape(B, 6, D)
    qkv = _qkv(x2, mod3, norm1_g[0].reshape(1, D), w_in[0].astype(BF16), B, S)
    qa, ka, va = qkv[:3]

    o_na = _na_attention(qa, ka, va, _na_bias_table(na_rpb[0], S), B, S).reshape(T, D_GRP)
    dil_outs = [_dil_branch(*qkv[3 + 3 * n:6 + 3 * n], B, S, d) for n, d in enumerate(DILATIONS)]

    x1, h2t, sct = _mid(x2, o_na, dil_outs, mod3,
                        out_norm_na_g[0].reshape(1, D_GRP), out_norm_dil_g[0].reshape(1, D_GRP),
                        w_out[0].astype(BF16), norm2_g[0].reshape(1, D),
                        peer_wq[0].T.astype(BF16), peer_subkeys[0].astype(BF16), S)
    sel = _select(sct)
    out = _peer(h2t, peer_u[0].astype(BF16), peer_v[0].T.astype(BF16), sel, x1, mod3,
                final_g.reshape(1, D), S)
    return out.reshape(B, S, D)
```

```python
import functools

import numpy as np
import jax
import jax.numpy as jnp
from jax import lax
from jax.experimental import pallas as pl
from jax.experimental.pallas import tpu as pltpu

F32 = jnp.float32
BF16 = jnp.bfloat16

D_MODEL = 1024
HEAD_DIM = 64
N_HEADS = 8
D_GRP = N_HEADS * HEAD_DIM
GRID_W = 64
NA_ROWS = 8
NA_COLS = 16
DIL_CONFIGS = ((128, 1), (512, 4), (2048, 16))
PEER_HEADS = 8
NKEYS = 128
TOPK = 16
EPS = 1e-6
MASKED = -1e30
LANES = 128
VMEM_LIMIT = 56 * 1024 * 1024

NA_QROWS = 4
NA_KROWS = 12
NA_QB = NA_QROWS * GRID_W
NA_KB = NA_KROWS * GRID_W
DIL_QB = 128
DIL_HALF = 64
DIL_GROUP = 4


def _cparams(*sem):
    return pltpu.CompilerParams(dimension_semantics=sem, vmem_limit_bytes=VMEM_LIMIT)


def _rms(x, g):
    ms = jnp.mean(x * x, axis=-1, keepdims=True)
    return x * lax.rsqrt(ms + EPS) * g


def _stack_heads(q):
    qf = q.astype(F32)
    lo = lax.broadcasted_iota(jnp.int32, qf.shape, 1) < HEAD_DIM
    return jnp.concatenate([jnp.where(lo, qf, 0.0), jnp.where(lo, 0.0, qf)], axis=0).astype(BF16)


def _dot_t(a, b):
    return lax.dot_general(a, b, (((1,), (1,)), ((), ())), preferred_element_type=F32)


def _mod_kernel(c_ref, w_ref, b_ref, o_ref):
    c = c_ref[...]
    s = c / (1.0 + jnp.exp(-c))
    o_ref[...] = jnp.dot(s, w_ref[...], preferred_element_type=F32,
                         precision=lax.Precision.HIGHEST) + b_ref[...]


def _modulation(c, ada_w, ada_b):
    B, D = c.shape
    n = ada_w.shape[1] // D
    return pl.pallas_call(
        _mod_kernel,
        grid=(n,),
        in_specs=[pl.BlockSpec((B, D), lambda j: (0, 0)),
                  pl.BlockSpec((D, D), lambda j: (0, j)),
                  pl.BlockSpec((1, D), lambda j: (0, j))],
        out_specs=pl.BlockSpec((B, D), lambda j: (0, j)),
        out_shape=jax.ShapeDtypeStruct((B, n * D), F32),
        compiler_params=_cparams("arbitrary"),
        name="mod",
    )(c, ada_w, ada_b.reshape(1, -1))


DILATIONS = tuple(d for _, d in DIL_CONFIGS)


def _qkv_kernel(x_ref, mod_ref, g_ref, w_ref, *refs):
    out_refs, stage_ref = refs[:-1], refs[-1]
    tm = x_ref.shape[0]
    sh1 = mod_ref[0, 0:1, :]
    sc1 = mod_ref[0, 1:2, :]
    h = _rms(x_ref[...], g_ref[...]) * (1.0 + sc1) + sh1
    hb = h.astype(BF16)
    for j in range(6):
        p = jnp.dot(hb, w_ref[:, j * D_GRP:(j + 1) * D_GRP], preferred_element_type=F32)
        if j % 3 == 0:
            p = p * (HEAD_DIM ** -0.5)
        if j < 3:
            out_refs[j][...] = p.astype(BF16)
            continue
        for c in range(D_GRP // LANES):
            stage_ref[c] = p[:, c * LANES:(c + 1) * LANES]
        for n, d in enumerate(DILATIONS):
            o_ref = out_refs[3 * n + j]
            for rho in range(d):
                for c in range(D_GRP // LANES):
                    col = rho * D_GRP + c * LANES
                    o_ref[0, :, col:col + LANES] = (
                        stage_ref[c, pl.ds(rho, tm // d, stride=d), :].astype(BF16))


def _qkv(x2, mod3, g, w_in_b, B, S):
    T, D = x2.shape
    tm = 512
    per_b = S // tm
    dil_specs, dil_shapes = [], []
    for d in DILATIONS:
        dil_specs += [pl.BlockSpec((1, tm // d, d * D_GRP), lambda i: (i // per_b, i % per_b, 0))] * 3
        dil_shapes += [jax.ShapeDtypeStruct((B, S // d, d * D_GRP), BF16)] * 3
    return pl.pallas_call(
        _qkv_kernel,
        grid=(T // tm,),
        in_specs=[pl.BlockSpec((tm, D), lambda i: (i, 0)),
                  pl.BlockSpec((1, 6, D), lambda i: (i // per_b, 0, 0)),
                  pl.BlockSpec((1, D), lambda i: (0, 0)),
                  pl.BlockSpec((D, 6 * D_GRP), lambda i: (0, 0))],
        out_specs=[pl.BlockSpec((tm, D_GRP), lambda i: (i, 0))] * 3 + dil_specs,
        out_shape=[jax.ShapeDtypeStruct((T, D_GRP), BF16)] * 3 + dil_shapes,
        scratch_shapes=[pltpu.VMEM((D_GRP // LANES, tm, LANES), F32)],
        compiler_params=_cparams("parallel"),
        name="qkv",
    )(x2, mod3, g, w_in_b)


def _na_bias_table(rpb, S):
    rows = S // GRID_W
    n_dr, n_dc = 2 * NA_ROWS - 1, 2 * NA_COLS - 1
    c = np.arange(GRID_W)[:, None]
    kc = np.arange(GRID_W)[None, :]
    c0 = np.clip(c - NA_COLS // 2, 0, GRID_W - NA_COLS)
    col_ok = (kc >= c0) & (kc < c0 + NA_COLS)
    dc = np.clip(kc - c + NA_COLS - 1, 0, n_dc - 1)
    onehot = ((dc[None] == np.arange(n_dc)[:, None, None]) & col_ok[None]).astype(np.float32)
    blocks = jnp.einsum('hrd,dck->hrck', rpb.astype(F32), jnp.asarray(onehot),
                        precision=lax.Precision.HIGHEST)
    blocks = jnp.where(jnp.asarray(col_ok)[None, None], blocks, MASKED)
    masked_block = jnp.full((rpb.shape[0], 1, GRID_W, GRID_W), MASKED, F32)
    blocks = jnp.concatenate([blocks, masked_block], axis=1)
    a = np.arange(NA_QROWS)[:, None]
    wr = np.arange(NA_KROWS)[None, :]
    idx = []
    for R in (0, 1, rows // NA_QROWS - 1):
        r = NA_QROWS * R + a
        r0 = np.clip(r - NA_ROWS // 2, 0, rows - NA_ROWS)
        w0 = int(np.clip(NA_QROWS * R - NA_ROWS // 2, 0, rows - NA_KROWS))
        krow = w0 + wr
        row_ok = (krow >= r0) & (krow < r0 + NA_ROWS)
        idx.append(np.where(row_ok, krow - r + NA_ROWS - 1, n_dr))
    t = jnp.take(blocks, jnp.asarray(np.stack(idx).reshape(-1), jnp.int32), axis=1)
    t = t.reshape(-1, 3, NA_QROWS, NA_KROWS, GRID_W, GRID_W)
    return t.transpose(1, 0, 2, 4, 3, 5).reshape(3, -1, NA_QB, NA_KB)


def _na_kernel(q_ref, k0_ref, k1_ref, k2_ref, v0_ref, v1_ref, v2_ref, bias_ref, o_ref):
    k_refs = (k0_ref, k1_ref, k2_ref)
    v_refs = (v0_ref, v1_ref, v2_ref)
    lo = lax.broadcasted_iota(jnp.int32, (NA_QB, LANES), 1) < HEAD_DIM
    kb = NA_KB // 3
    for p in range(N_HEADS // 2):
        sl = slice(LANES * p, LANES * (p + 1))
        qs = _stack_heads(q_ref[0, :, sl])
        s = jnp.concatenate([_dot_t(qs, kr[0, :, sl]) for kr in k_refs], axis=1)
        s = s + jnp.concatenate([bias_ref[0, 2 * p], bias_ref[0, 2 * p + 1]], axis=0)
        m = jnp.max(s, axis=1, keepdims=True)
        e = jnp.exp(s - m)
        l = jnp.sum(e, axis=1, keepdims=True)
        eb = e.astype(BF16)
        o = jnp.dot(eb[:, 0:kb], v_refs[0][0, :, sl], preferred_element_type=F32)
        for j in (1, 2):
            o = o + jnp.dot(eb[:, j * kb:(j + 1) * kb], v_refs[j][0, :, sl],
                            preferred_element_type=F32)
        o = o / l
        o_ref[0, :, sl] = jnp.where(lo, o[:NA_QB], o[NA_QB:]).astype(BF16)


def _na_attention(q, k, v, bias, B, S):
    nblk = S // NA_QB
    kblk = NA_KB // 3
    assert kblk == NA_QB
    q3, k3, v3 = (t.reshape(B, S, D_GRP) for t in (q, k, v))

    def w0(R):
        return jnp.clip(R - 1, 0, nblk - 3)

    def kv_spec(j):
        return pl.BlockSpec((1, kblk, D_GRP), lambda R, b: (b, w0(R) + j, 0))

    return pl.pallas_call(
        _na_kernel,
        grid=(nblk, B),
        in_specs=[pl.BlockSpec((1, NA_QB, D_GRP), lambda R, b: (b, R, 0)),
                  kv_spec(0), kv_spec(1), kv_spec(2), kv_spec(0), kv_spec(1), kv_spec(2),
                  pl.BlockSpec((1, N_HEADS, NA_QB, NA_KB),
                               lambda R, b: (jnp.minimum(R, 1) + R // (nblk - 1), 0, 0, 0))],
        out_specs=pl.BlockSpec((1, NA_QB, D_GRP), lambda R, b: (b, R, 0)),
        out_shape=jax.ShapeDtypeStruct((B, S, D_GRP), BF16),
        compiler_params=_cparams("arbitrary", "arbitrary"),
        name="na",
    )(q3, k3, k3, k3, v3, v3, v3, bias)


def _alibi_slopes(n):
    return [float(2.0 ** (-8.0 * (i + 1) / n)) for i in range(n)]


def _dil_window(L):
    return min(L, DIL_QB + 2 * DIL_HALF)


def _dil_window_start(u0, L, clip):
    return clip(u0 - DIL_HALF, 0, L - _dil_window(L))


def _dil_dist_table(L, dilation):
    wn = _dil_window(L)
    nqb = L // DIL_QB
    tab = np.empty((nqb, DIL_QB, wn), np.float32)
    for qb in range(nqb):
        ks = int(_dil_window_start(qb * DIL_QB, L, np.clip))
        uq = qb * DIL_QB + np.arange(DIL_QB)[:, None]
        uk = ks + np.arange(wn)[None, :]
        delta = np.abs(uk - uq)
        tab[qb] = np.where(delta <= DIL_HALF, -float(dilation) * delta, MASKED)
    return jnp.asarray(tab)


def _dil_kernel(q_ref, k_ref, v_ref, d_ref, o_ref, lse_ref, *, L, group, slopes):
    wn = _dil_window(L)
    lane = lax.broadcasted_iota(jnp.int32, (DIL_QB, LANES), 1)
    lo = lane < HEAD_DIM

    def block(qb):
        u0 = pl.multiple_of(qb * DIL_QB, DIL_QB)
        ks = pl.multiple_of(_dil_window_start(u0, L, jnp.clip), DIL_HALF)
        dist = d_ref[qb]
        for g in range(group):
            lse_tile = jnp.zeros((DIL_QB, LANES), F32)
            for p in range(N_HEADS // 2):
                sl = slice(g * D_GRP + LANES * p, g * D_GRP + LANES * (p + 1))
                qs = _stack_heads(q_ref[0, pl.ds(u0, DIL_QB), sl])
                s = _dot_t(qs, k_ref[0, pl.ds(ks, wn), sl])
                s = s + jnp.concatenate([slopes[2 * p] * dist, slopes[2 * p + 1] * dist], axis=0)
                m = jnp.max(s, axis=1, keepdims=True)
                e = jnp.exp(s - m)
                l = jnp.sum(e, axis=1, keepdims=True)
                o = jnp.dot(e.astype(BF16), v_ref[0, pl.ds(ks, wn), sl],
                            preferred_element_type=F32) / l
                lse = jnp.broadcast_to(m + jnp.log(l), (2 * DIL_QB, LANES))
                o_ref[0, pl.ds(u0, DIL_QB), sl] = jnp.where(lo, o[:DIL_QB], o[DIL_QB:])
                lse_tile = jnp.where(lane == 2 * p, lse[:DIL_QB], lse_tile)
                lse_tile = jnp.where(lane == 2 * p + 1, lse[DIL_QB:], lse_tile)
            lse_ref[0, pl.ds(u0, DIL_QB), g * LANES:(g + 1) * LANES] = lse_tile

    unroll = min(max(DIL_GROUP // group, 1), L // DIL_QB)

    def body(i, carry):
        for j in range(unroll):
            block(i * unroll + j)
        return carry

    lax.fori_loop(0, L // DIL_QB // unroll, body, 0)


def _dil_branch(q, k, v, B, S, dilation):
    L = S // dilation
    wn = _dil_window(L)
    group = min(dilation, DIL_GROUP)
    spec = pl.BlockSpec((1, L, group * D_GRP), lambda b, rho: (b, 0, rho))
    dist = _dil_dist_table(L, dilation)
    return pl.pallas_call(
        functools.partial(_dil_kernel, L=L, group=group, slopes=_alibi_slopes(N_HEADS)),
        grid=(B, dilation // group),
        in_specs=[spec, spec, spec,
                  pl.BlockSpec((L // DIL_QB, DIL_QB, wn), lambda b, rho: (0, 0, 0))],
        out_specs=[spec, pl.BlockSpec((1, L, group * LANES), lambda b, rho: (b, 0, rho))],
        out_shape=[jax.ShapeDtypeStruct((B, L, dilation * D_GRP), F32),
                   jax.ShapeDtypeStruct((B, L, dilation * LANES), F32)],
        compiler_params=_cparams("parallel", "arbitrary"),
        name=f"dil{dilation}",
    )(q, k, v, dist)


def _mid_kernel(x_ref, ona_ref, o1_ref, o2_ref, o3_ref, l1_ref, l2_ref, l3_ref, mod_ref,
                gna_ref, gdil_ref, wout_ref, g2_ref, wqt_ref, sk_ref,
                x1_ref, h2t_ref, sct_ref, *stage_refs):
    tm = x_ref.shape[0]

    def token_order(ref, d, stage_ref):
        w = ref.shape[2] // d
        if d == 1:
            return [ref[0, :, c * LANES:(c + 1) * LANES] for c in range(w // LANES)]
        for rho in range(d):
            for c in range(w // LANES):
                col = rho * w + c * LANES
                stage_ref[c, pl.ds(rho, tm // d, stride=d), :] = ref[0, :, col:col + LANES]
        return [stage_ref[c] for c in range(w // LANES)]

    o_stage, l_stage = stage_refs[:len(DILATIONS)], stage_refs[len(DILATIONS):]
    o1, o2, o3 = (token_order(r, d, s) for r, d, s in zip((o1_ref, o2_ref, o3_ref), DILATIONS, o_stage))
    l1, l2, l3 = (token_order(r, d, s)[0] for r, d, s in zip((l1_ref, l2_ref, l3_ref), DILATIONS, l_stage))
    mx = jnp.maximum(jnp.maximum(l1, l2), l3)
    e1, e2, e3 = jnp.exp(l1 - mx), jnp.exp(l2 - mx), jnp.exp(l3 - mx)
    inv = 1.0 / (e1 + e2 + e3)
    alphas = (e1 * inv, e2 * inv, e3 * inv)
    lo = lax.broadcasted_iota(jnp.int32, (tm, LANES), 1) < HEAD_DIM
    parts = []
    for p in range(N_HEADS // 2):
        sl = slice(LANES * p, LANES * (p + 1))
        acc = jnp.zeros((tm, LANES), F32)
        for a, o in zip(alphas, (o1, o2, o3)):
            w = jnp.where(lo, a[:, 2 * p:2 * p + 1], a[:, 2 * p + 1:2 * p + 2])
            acc = acc + w * o[p]
        parts.append(acc)
    o_dil = jnp.concatenate(parts, axis=1)

    na_n = _rms(ona_ref[...].astype(F32), gna_ref[...]).astype(BF16)
    dil_n = _rms(o_dil, gdil_ref[...]).astype(BF16)
    y = jnp.dot(na_n, wout_ref[0:D_GRP, :], preferred_element_type=F32)
    y = y + jnp.dot(dil_n, wout_ref[D_GRP:2 * D_GRP, :], preferred_element_type=F32)
    g1 = mod_ref[0, 2:3, :]
    sh2 = mod_ref[0, 3:4, :]
    sc2 = mod_ref[0, 4:5, :]
    x1 = x_ref[...] + g1 * y
    x1_ref[...] = x1
    h2 = _rms(x1, g2_ref[...]) * (1.0 + sc2) + sh2
    h2t = h2.T.astype(BF16)
    h2t_ref[...] = h2t
    qpt = jnp.dot(wqt_ref[...], h2t, preferred_element_type=F32).astype(BF16)
    for hp in range(2 * PEER_HEADS):
        rows = slice(NKEYS * hp, NKEYS * (hp + 1))
        sct_ref[rows, :] = jnp.dot(sk_ref[hp % 2], qpt[rows, :], preferred_element_type=F32)


def _mid(x2, o_na, dil_outs, mod3, g_na, g_dil, w_out_b, g2, wq_t, subk, S):
    T, D = x2.shape
    tm = 512
    per_b = S // tm
    nq = wq_t.shape[0]
    row = lambda w: pl.BlockSpec((tm, w), lambda i: (i, 0))
    full = lambda s: pl.BlockSpec(s, lambda i: (0,) * len(s))
    view = lambda d, w: pl.BlockSpec((1, tm // d, d * w), lambda i: (i // per_b, i % per_b, 0))
    (o1, l1), (o2, l2), (o3, l3) = dil_outs
    return pl.pallas_call(
        _mid_kernel,
        grid=(T // tm,),
        in_specs=[row(D), row(D_GRP)] + [view(d, D_GRP) for d in DILATIONS]
                 + [view(d, LANES) for d in DILATIONS]
                 + [pl.BlockSpec((1, 6, D), lambda i: (i // per_b, 0, 0)),
                  full((1, D_GRP)), full((1, D_GRP)), full((D, D)), full((1, D)),
                  full((nq, D)), full((2, NKEYS, NKEYS))],
        out_specs=[row(D),
                   pl.BlockSpec((D, tm), lambda i: (0, i)),
                   pl.BlockSpec((nq, tm), lambda i: (0, i))],
        out_shape=[jax.ShapeDtypeStruct((T, D), F32),
                   jax.ShapeDtypeStruct((D, T), BF16),
                   jax.ShapeDtypeStruct((nq, T), F32)],
        scratch_shapes=[pltpu.VMEM((D_GRP // LANES, tm, LANES), F32)] * len(DILATIONS)
                       + [pltpu.VMEM((1, tm, LANES), F32)] * len(DILATIONS),
        compiler_params=_cparams("parallel"),
        name="mid",
    )(x2, o_na, o1, o2, o3, l1, l2, l3, mod3, g_na, g_dil, w_out_b, g2, wq_t, subk)


SUBLANES = 8
SEL_LANES = 128
NOT_TOP = float(TOPK)


def _merge_exchange_network(n):
    pairs = []
    t = (n - 1).bit_length()
    p = 1 << (t - 1)
    while p > 0:
        q, r, d = 1 << (t - 1), 0, p
        while d > 0:
            pairs += [(i, i + d) for i in range(n - d) if (i & p) == r]
            d, q, r = q - p, q >> 1, p
        p >>= 1
    return tuple(pairs)


def _bitonic_merge_network(n):
    pairs = []
    d = n // 2
    while d > 0:
        pairs += [(i, i + d) for i in range(n) if (i & d) == 0]
        d //= 2
    return tuple(pairs)


_SORT16 = _merge_exchange_network(TOPK)
_BITONIC16 = _bitonic_merge_network(TOPK)


def _compare_exchange(v, network):
    for i, j in network:
        v[i], v[j] = jnp.maximum(v[i], v[j]), jnp.minimum(v[i], v[j])


def _top16_values(v):
    v = list(v)
    _compare_exchange(v, _SORT16)
    for shift in (4, 2, 1):
        v = [jnp.maximum(v[k], pltpu.roll(v[TOPK - 1 - k], shift, axis=0)) for k in range(TOPK)]
        _compare_exchange(v, _BITONIC16)
    return v


def _all_sublanes_sum(x):
    for shift in (4, 2, 1):
        x = x + pltpu.roll(x, shift, axis=0)
    return x


def _select_tile_fast(s1, s2):
    n = s1[0].shape[1]
    sub = lax.broadcasted_iota(jnp.int32, (SUBLANES, n), 0)
    a = _top16_values(s1)
    b = _top16_values(s2)

    def spread(vals):
        out = vals[0]
        for k in range(1, SUBLANES):
            out = jnp.where(sub == k, vals[k], out)
        return out

    a_lo, a_hi = spread(a[:SUBLANES]), spread(a[SUBLANES:])
    b_lo, b_hi = spread(b[:SUBLANES]), spread(b[SUBLANES:])
    a_mid = jnp.where(sub < 4, -jnp.inf, a_lo)
    cand = []
    for r in range(4):
        cand += [a[r] + b_lo, a[r] + b_hi]
    cand += [a_mid + b[0], a_hi + b[0], a_mid + b[1], a_mid + b[2]]
    pad = [jnp.full((SUBLANES, n), -jnp.inf, F32)] * (TOPK - len(cand))
    thr = _top16_values(cand + pad)[TOPK - 1]
    m0 = a[0] + b[0]
    picked = [jnp.where(cv >= thr, 1.0, 0.0) for cv in cand]
    z = picked[0] * jnp.exp(cand[0] - m0)
    n_picked = picked[0]
    for pk, cv in zip(picked[1:], cand[1:]):
        z = z + pk * jnp.exp(cv - m0)
        n_picked = n_picked + pk
    inv_z = 1.0 / _all_sublanes_sum(z)
    counts = [_all_sublanes_sum(picked[2 * r] + picked[2 * r + 1]) for r in range(4)]
    n_mid = picked[8] + picked[10] + picked[11]
    counts += [_all_sublanes_sum(jnp.where(sub == r, n_mid, 0.0)) for r in range(4, 8)]
    counts += [_all_sublanes_sum(jnp.where(sub == r - 8, picked[9], 0.0)) for r in range(8, 16)]

    flag = jnp.where(_all_sublanes_sum(n_picked) != float(TOPK), 1.0, 0.0)
    for vals, keys, edge_used in ((a, s1, counts[TOPK - 1] > 0.0), (b, s2, counts[0] >= float(TOPK))):
        n_top = jnp.where(keys[0] >= vals[TOPK - 1], 1.0, 0.0)
        for x in keys[1:]:
            n_top = n_top + jnp.where(x >= vals[TOPK - 1], 1.0, 0.0)
        edge_tie = _all_sublanes_sum(n_top) != float(TOPK)
        flag = jnp.where(edge_tie & edge_used, 1.0, flag)

    count, c = [], []
    for x in s1:
        cnt = jnp.zeros(x.shape, F32)
        for r in range(TOPK):
            cnt = jnp.where(x == a[r], counts[r], cnt)
        count.append(cnt)
        c.append(jnp.exp(x - a[0]) * inv_z)
    rank2, f = [], []
    for x in s2:
        rk = jnp.full(x.shape, NOT_TOP, F32)
        for cc in range(TOPK - 1, -1, -1):
            rk = jnp.where(b[cc] <= x, float(cc), rk)
        rank2.append(rk)
        f.append(jnp.exp(x - b[0]))
    return rank2, f, count, c, flag


def _top16_exact(s, iota):
    work = s
    rank = jnp.full(s.shape, NOT_TOP, F32)
    vals = []
    for r in range(TOPK):
        m = jnp.max(work, axis=0, keepdims=True)
        first = jnp.min(jnp.where(work == m, iota, float(s.shape[0])), axis=0, keepdims=True)
        sel = iota == first
        rank = jnp.where(sel, float(r), rank)
        work = jnp.where(sel, -jnp.inf, work)
        vals.append(m)
    return vals, rank


def _select_tile_exact(s1, s2):
    n = s1.shape[1]
    iota_k = lax.broadcasted_iota(jnp.int32, (NKEYS, n), 0).astype(F32)
    iota_c = lax.broadcasted_iota(jnp.int32, (TOPK * TOPK, n), 0).astype(F32)
    iota_t = lax.broadcasted_iota(jnp.int32, (TOPK, n), 0)
    a, rank1 = _top16_exact(s1, iota_k)
    b, rank2 = _top16_exact(s2, iota_k)
    b16 = jnp.zeros((TOPK, n), F32)
    for cc in range(TOPK):
        b16 = jnp.where(iota_t == cc, b[cc], b16)
    cand = jnp.concatenate([a[r] + b16 for r in range(TOPK)], axis=0)
    work = cand
    picked = jnp.zeros(cand.shape, F32)
    for _ in range(TOPK):
        m = jnp.max(work, axis=0, keepdims=True)
        first = jnp.min(jnp.where(work == m, iota_c, float(TOPK * TOPK)), axis=0, keepdims=True)
        sel = iota_c == first
        picked = jnp.where(sel, 1.0, picked)
        work = jnp.where(sel, -jnp.inf, work)
    z = jnp.sum(picked * jnp.exp(cand - (a[0] + b[0])), axis=0, keepdims=True)
    count = jnp.zeros((NKEYS, n), F32)
    for r in range(TOPK):
        n_r = jnp.sum(picked[r * TOPK:(r + 1) * TOPK], axis=0, keepdims=True)
        count = jnp.where(rank1 == float(r), n_r, count)
    return rank2, jnp.exp(s2 - b[0]), count, jnp.exp(s1 - a[0]) / z


def _select_kernel(sc_ref, r2_ref, f_ref, cnt_ref, c_ref):
    tb = sc_ref.shape[1]
    vregs = NKEYS // SUBLANES
    flags = []
    for lt in range(tb // SEL_LANES):
        lanes = slice(lt * SEL_LANES, (lt + 1) * SEL_LANES)

        def fast(h, flag, lanes=lanes):
            base = pl.multiple_of(h * 2 * NKEYS, 2 * NKEYS)
            out = pl.multiple_of(h * NKEYS, NKEYS)
            s1 = [sc_ref[pl.ds(base + SUBLANES * k, SUBLANES), lanes] for k in range(vregs)]
            s2 = [sc_ref[pl.ds(base + NKEYS + SUBLANES * k, SUBLANES), lanes] for k in range(vregs)]
            rank2, f, count, c, bad = _select_tile_fast(s1, s2)
            for k in range(0, vregs, 2):
                rows = pl.ds(out + SUBLANES * k, 2 * SUBLANES)
                r2_ref[rows, lanes] = jnp.concatenate(rank2[k:k + 2], axis=0).astype(BF16)
                f_ref[rows, lanes] = jnp.concatenate(f[k:k + 2], axis=0).astype(BF16)
            for k in range(vregs):
                rows = pl.ds(out + SUBLANES * k, SUBLANES)
                cnt_ref[rows, lanes] = count[k]
                c_ref[rows, lanes] = c[k]
            return jnp.maximum(flag, bad)

        flags.append(lax.fori_loop(0, PEER_HEADS, fast, jnp.zeros((SUBLANES, SEL_LANES), F32)))

    for lt in range(tb // SEL_LANES):
        lanes = slice(lt * SEL_LANES, (lt + 1) * SEL_LANES)

        @pl.when(jnp.max(flags[lt]) > 0.0)
        def _(lanes=lanes):

            def exact(h, carry, lanes=lanes):
                base = pl.multiple_of(h * 2 * NKEYS, 2 * NKEYS)
                rows = pl.ds(pl.multiple_of(h * NKEYS, NKEYS), NKEYS)
                rank2, f, count, c = _select_tile_exact(sc_ref[pl.ds(base, NKEYS), lanes],
                                                        sc_ref[pl.ds(base + NKEYS, NKEYS), lanes])
                r2_ref[rows, lanes] = rank2.astype(BF16)
                f_ref[rows, lanes] = f.astype(BF16)
                cnt_ref[rows, lanes] = count
                c_ref[rows, lanes] = c
                return carry

            lax.fori_loop(0, PEER_HEADS, exact, 0)


def _select(sct):
    nq, T = sct.shape
    tb = 512
    rows = PEER_HEADS * NKEYS
    spec = pl.BlockSpec((rows, tb), lambda i: (0, i))
    return pl.pallas_call(
        _select_kernel,
        grid=(T // tb,),
        in_specs=[pl.BlockSpec((nq, tb), lambda i: (0, i))],
        out_specs=[spec] * 4,
        out_shape=[jax.ShapeDtypeStruct((rows, T), BF16)] * 2 + [jax.ShapeDtypeStruct((rows, T), F32)] * 2,
        compiler_params=_cparams("parallel"),
        name="select",
    )(sct)


PEER_ROWS_PER_STEP = 16
PEER_SUB_ROWS = 4
PEER_LOOKAHEAD = 3
BF16_ROWS = 16
PEER_GATE_LANES = 256


def _peer_kernel(h2t_ref, u_ref, vt_ref, r2_ref, f_ref, cnt_ref, c_ref, x1_ref, mod_ref, fg_ref,
                 o_ref, acc_ref, a_ref, w_ref):
    n = pl.program_id(1)
    tb = h2t_ref.shape[1]

    @pl.when(n == 0)
    def _():
        acc_ref[...] = jnp.zeros_like(acc_ref)

    sub = PEER_SUB_ROWS * NKEYS
    n_sub = PEER_ROWS_PER_STEP // PEER_SUB_ROWS
    chunk = lambda s: slice(s * sub, (s + 1) * sub)

    def scores(s):
        a_ref[chunk(s), :] = jnp.dot(u_ref[chunk(s), :], h2t_ref[...],
                                     preferred_element_type=F32)

    def gate(s):
        for k in range(s * PEER_SUB_ROWS, (s + 1) * PEER_SUB_ROWS):
            for lt in range(tb // PEER_GATE_LANES):
                lanes = slice(lt * PEER_GATE_LANES, (lt + 1) * PEER_GATE_LANES)
                bcast = lambda ref, h, scale: jnp.broadcast_to(
                    ref[h, k:k + 1, lanes] * scale, (BF16_ROWS, PEER_GATE_LANES)).astype(BF16)
                cnt = [bcast(cnt_ref, h, 1.0) for h in range(PEER_HEADS)]
                cc = [bcast(c_ref, h, 0.5) for h in range(PEER_HEADS)]
                for q in range(NKEYS // BF16_ROWS):
                    rows = slice(k * NKEYS + q * BF16_ROWS, k * NKEYS + (q + 1) * BF16_ROWS)
                    g = None
                    for h in range(PEER_HEADS):
                        keys = slice(h * NKEYS + q * BF16_ROWS, h * NKEYS + (q + 1) * BF16_ROWS)
                        term = jnp.where(r2_ref[keys, lanes] < cnt[h], f_ref[keys, lanes], 0.0) * cc[h]
                        g = term if g is None else g + term
                    av = a_ref[rows, lanes]
                    twice_gelu = av * (1.0 + lax.erf(av * (2.0 ** -0.5)))
                    w_ref[rows, lanes] = g * twice_gelu.astype(BF16)

    def combine(s):
        acc_ref[...] += jnp.dot(vt_ref[:, chunk(s)], w_ref[chunk(s), :],
                                preferred_element_type=F32)

    for s in range(PEER_LOOKAHEAD):
        scores(s)
    for s in range(n_sub):
        if s + PEER_LOOKAHEAD < n_sub:
            scores(s + PEER_LOOKAHEAD)
        gate(s)
        combine(s)

    @pl.when(n == pl.num_programs(1) - 1)
    def _():
        g2 = mod_ref[0, 5:6, :]
        x2 = x1_ref[...] + g2 * acc_ref[...].T
        o_ref[...] = _rms(x2, fg_ref[...])


def _peer(h2t, u_b, vt_b, sel, x1, mod3, fg, S):
    D, T = h2t.shape
    tb = 512
    ic = PEER_ROWS_PER_STEP
    nc = ic * NKEYS
    per_b = S // tb
    rows = PEER_HEADS * NKEYS
    r2, f, cnt, c = sel
    tok = lambda r: pl.BlockSpec((r, tb), lambda t, n: (0, t))
    per_row = pl.BlockSpec((PEER_HEADS, ic, tb), lambda t, n: (0, n, t))
    return pl.pallas_call(
        _peer_kernel,
        grid=(T // tb, NKEYS // ic),
        in_specs=[tok(D),
                  pl.BlockSpec((nc, D), lambda t, n: (n, 0)),
                  pl.BlockSpec((D, nc), lambda t, n: (0, n)),
                  tok(rows), tok(rows), per_row, per_row,
                  pl.BlockSpec((tb, D), lambda t, n: (t, 0)),
                  pl.BlockSpec((1, 6, D), lambda t, n: (t // per_b, 0, 0)),
                  pl.BlockSpec((1, D), lambda t, n: (0, 0))],
        out_specs=pl.BlockSpec((tb, D), lambda t, n: (t, 0)),
        out_shape=jax.ShapeDtypeStruct((T, D), F32),
        scratch_shapes=[pltpu.VMEM((D, tb), F32), pltpu.VMEM((nc, tb), F32),
                        pltpu.VMEM((nc, tb), BF16)],
        compiler_params=_cparams("parallel", "arbitrary"),
        name="peer",
    )(h2t, u_b, vt_b, r2, f, cnt.reshape(PEER_HEADS, NKEYS, T), c.reshape(PEER_HEADS, NKEYS, T),
      x1, mod3, fg)


def kernel(x, c, ada_w, ada_b, norm1_g, w_in, na_rpb, out_norm_na_g, out_norm_dil_g, w_out,
           norm2_g, peer_wq, peer_subkeys, peer_u, peer_v, final_g):
    B, S, D = x.shape
    assert ada_w.shape[0] == 1, "single-layer block"
    T = B * S
    x2 = x.reshape(T, D)

    mod3 = _modulation(c, ada_w[0], ada_b[0]).reshape(B, 6, D)
    qkv = _qkv(x2, mod3, norm1_g[0].reshape(1, D), w_in[0].astype(BF16), B, S)
    qa, ka, va = qkv[:3]

    o_na = _na_attention(qa, ka, va, _na_bias_table(na_rpb[0], S), B, S).reshape(T, D_GRP)
    dil_outs = [_dil_branch(*qkv[3 + 3 * n:6 + 3 * n], B, S, d) for n, d in enumerate(DILATIONS)]

    x1, h2t, sct = _mid(x2, o_na, dil_outs, mod3,
                        out_norm_na_g[0].reshape(1, D_GRP), out_norm_dil_g[0].reshape(1, D_GRP),
                        w_out[0].astype(BF16), norm2_g[0].reshape(1, D),
                        peer_wq[0].T.astype(BF16), peer_subkeys[0].astype(BF16), S)
    sel = _select(sct)
    out = _peer(h2t, peer_u[0].astype(BF16), peer_v[0].T.astype(BF16), sel, x1, mod3,
                final_g.reshape(1, D), S)
    return out.reshape(B, S, D)
```
